```python
import jax, jax.numpy as jnp
from jax import lax
import numpy as np

D_MODEL = 2048
BATCH = 4
SEQ = 2048
DEPTH = 2

N_MIXERS = 2
N_META = 16
D_FF = 5632
EPS = 1e-6
GLA_HEADS = 4
GLA_DK = D_MODEL // 2
GLA_DV = D_MODEL
GLA_HEAD_K = GLA_DK // GLA_HEADS
GLA_HEAD_V = GLA_DV // GLA_HEADS
GLA_GATE_RANK = 16
GLA_GATE_NORM = 16.0
GLA_CHUNK = 64
GLA_IN_W = GLA_DK + GLA_DK + GLA_DV + GLA_GATE_RANK + GLA_DV
POOL_WINDOWS = (2, 4, 8, 16)
POOL_GROUPS = 4
POOL_GROUP_W = D_MODEL // POOL_GROUPS
N_GLA_LAYERS = (DEPTH + 1) // 2
N_POOL_LAYERS = DEPTH // 2

kernel_name = 'hybrid_gla_pool_macaron'


def rms_norm(x, g):
    xf = x.astype(jnp.float32)
    y = xf * lax.rsqrt(jnp.mean(xf * xf, axis=-1, keepdims=True) + EPS)
    return (y * g.astype(jnp.float32)).astype(x.dtype)


def ffn_half(x, g, w_gate, w_up, w_down):
    h = rms_norm(x, g)
    return x + 0.5 * ((jax.nn.silu(h @ w_gate) * (h @ w_up)) @ w_down)


def gla_chunked(q, k, v, lg):
    B, H, T, dk = q.shape
    dv = v.shape[-1]
    C = GLA_CHUNK
    n = T // C

    def to_chunks(a):
        return jnp.moveaxis(a.reshape(B, H, n, C, a.shape[-1]), 2, 0)

    causal = jnp.tril(jnp.ones((C, C), dtype=bool))

    def step(S, inp):
        qc, kc, vc, gc = inp
        b = jnp.cumsum(gc, axis=2)
        b_last = b[:, :, -1:, :]
        o_inter = jnp.einsum('bhik,bhkv->bhiv', qc * jnp.exp(b), S)
        diff = b[:, :, :, None, :] - b[:, :, None, :, :]
        decay = jnp.exp(jnp.where(causal[:, :, None], diff, -jnp.inf))
        A = jnp.einsum('bhijk,bhjk->bhij', qc[:, :, :, None, :] * decay, kc)
        o = o_inter + jnp.einsum('bhij,bhjv->bhiv', A, vc)
        S = jnp.exp(b_last[:, :, 0, :])[..., None] * S + jnp.einsum('bhjk,bhjv->bhkv', kc * jnp.exp(b_last - b), vc)
        return S, o

    S0 = jnp.zeros((B, H, dk, dv), jnp.float32)
    _, o = lax.scan(step, S0, (to_chunks(q), to_chunks(k), to_chunks(v), to_chunks(lg)))
    return jnp.moveaxis(o, 0, 2).reshape(B, H, T, dv)


def gla_mixer(h, w_in, w_lr, b_lr, head_norm, w_out):
    B, L, _ = h.shape
    proj = h @ w_in
    q, k, v, lr, r = jnp.split(proj, [GLA_DK, 2 * GLA_DK, 2 * GLA_DK + GLA_DV, 2 * GLA_DK + GLA_DV + GLA_GATE_RANK], axis=-1)
    lg = jax.nn.log_sigmoid((lr @ w_lr + b_lr).astype(jnp.float32)) / GLA_GATE_NORM

    def heads(a, d):
        return a.reshape(B, L, GLA_HEADS, d).transpose(0, 2, 1, 3).astype(jnp.float32)

    q = heads(q, GLA_HEAD_K) * (GLA_HEAD_K ** -0.5)
    k = heads(k, GLA_HEAD_K)
    v = heads(v, GLA_HEAD_V)
    lg = heads(lg, GLA_HEAD_K)
    pad = (-N_META) % GLA_CHUNK
    padf = lambda a: jnp.pad(a, ((0, 0), (0, 0), (pad, 0), (0, 0)))
    o = gla_chunked(padf(q), padf(k), padf(v), padf(lg))[:, :, pad:, :]
    o = o * lax.rsqrt(jnp.mean(o * o, axis=-1, keepdims=True) + EPS) * head_norm.astype(jnp.float32)
    o = o.transpose(0, 2, 1, 3).reshape(B, L, GLA_DV).astype(h.dtype)
    return (o * jax.nn.silu(r)) @ w_out


def pool_mixer(h, w, b, scale):
    B, L, D = h.shape
    hf = h.astype(jnp.float32).reshape(B, L, POOL_GROUPS, POOL_GROUP_W)
    cs = jnp.cumsum(hf, axis=1)
    t = jnp.arange(L)
    outs = []
    for g, win in enumerate(POOL_WINDOWS):
        csg = cs[:, :, g]
        prev = jnp.pad(csg, ((0, 0), (win, 0), (0, 0)))[:, :L]
        cnt = jnp.minimum(t + 1, win).astype(jnp.float32)[:, None]
        outs.append((csg - prev) / cnt - hf[:, :, g])
    pooled = jnp.stack(outs, axis=2).astype(h.dtype)
    y = jnp.einsum('blgc,gcd->blgd', pooled, w) + b
    return y.reshape(B, L, D) * scale


def setup_inputs(seed: int = 0) -> dict:
    key = jax.random.key(seed)
    ks = jax.random.split(key, 20)
    f32 = jnp.float32
    nrm = lambda k, s, sc: jax.random.normal(k, s, f32) * sc
    return {
        'x': nrm(ks[0], (BATCH, SEQ, D_MODEL), 1.0),
        'meta': nrm(ks[1], (N_META, D_MODEL), 1.0),
        'ffn_norm': 1.0 + nrm(ks[2], (DEPTH, 2, D_MODEL), 0.02),
        'ffn_w_gate': nrm(ks[3], (DEPTH, 2, D_MODEL, D_FF), D_MODEL ** -0.5),
        'ffn_w_up': nrm(ks[4], (DEPTH, 2, D_MODEL, D_FF), D_MODEL ** -0.5),
        'ffn_w_down': nrm(ks[5], (DEPTH, 2, D_FF, D_MODEL), D_FF ** -0.5),
        'gla_norm': 1.0 + nrm(ks[6], (N_GLA_LAYERS, D_MODEL), 0.02),
        'gla_w_in': nrm(ks[7], (N_GLA_LAYERS, D_MODEL, GLA_IN_W), D_MODEL ** -0.5),
        'gla_w_lr': nrm(ks[8], (N_GLA_LAYERS, GLA_GATE_RANK, GLA_DK), GLA_GATE_RANK ** -0.5),
        'gla_b_lr': nrm(ks[9], (N_GLA_LAYERS, GLA_DK), 0.01),
        'gla_head_norm': 1.0 + nrm(ks[10], (N_GLA_LAYERS, GLA_HEAD_V), 0.02),
        'gla_w_out': nrm(ks[11], (N_GLA_LAYERS, GLA_DV, D_MODEL), GLA_DV ** -0.5),
        'pool_norm': 1.0 + nrm(ks[12], (N_POOL_LAYERS, D_MODEL), 0.02),
        'pool_w': nrm(ks[13], (N_POOL_LAYERS, POOL_GROUPS, POOL_GROUP_W, POOL_GROUP_W), POOL_GROUP_W ** -0.5),
        'pool_b': nrm(ks[14], (N_POOL_LAYERS, POOL_GROUPS, POOL_GROUP_W), 0.01),
        'pool_scale': 1.0 + nrm(ks[15], (N_POOL_LAYERS, D_MODEL), 0.02),
        'final_norm': 1.0 + nrm(ks[16], (D_MODEL,), 0.02),
    }


def reference(x, meta, ffn_norm, ffn_w_gate, ffn_w_up, ffn_w_down, gla_norm, gla_w_in, gla_w_lr, gla_b_lr,
              gla_head_norm, gla_w_out, pool_norm, pool_w, pool_b, pool_scale, final_norm):
    B = x.shape[0]
    m = jnp.broadcast_to(meta.astype(x.dtype)[None], (B, N_META, D_MODEL))
    x = jnp.concatenate([m, x], axis=1)
    for i in range(DEPTH):
        x = ffn_half(x, ffn_norm[i, 0], ffn_w_gate[i, 0], ffn_w_up[i, 0], ffn_w_down[i, 0])
        j = i // N_MIXERS
        if i % N_MIXERS == 0:
            x = x + gla_mixer(rms_norm(x, gla_norm[j]), gla_w_in[j], gla_w_lr[j], gla_b_lr[j], gla_head_norm[j], gla_w_out[j])
        else:
            x = x + pool_mixer(rms_norm(x, pool_norm[j]), pool_w[j], pool_b[j], pool_scale[j])
        x = ffn_half(x, ffn_norm[i, 1], ffn_w_gate[i, 1], ffn_w_up[i, 1], ffn_w_down[i, 1])
    return rms_norm(x, final_norm)[:, N_META:]
```

```python
import functools

import jax
import jax.numpy as jnp
import numpy as np
from jax import lax
from jax.experimental import pallas as pl
from jax.experimental.pallas import tpu as pltpu

F32 = jnp.float32
BF16 = jnp.bfloat16

EPS = 1e-6
N_META = 16
GLA_HEADS = 4
GLA_GATE_RANK = 16
GLA_GATE_NORM = 16.0
POOL_WINDOWS = (2, 4, 8, 16)

V7X_LANES = 128
V7X_VMEM_LIMIT_BYTES = 56 * 1024 * 1024

ROW_TILE = 688
FF_TILE = 512
GLA_CHUNK = 256
GLA_IN_TILE = 1024
POOL_HALO = 16


def _params(*semantics):
    return pltpu.CompilerParams(dimension_semantics=semantics,
                                vmem_limit_bytes=V7X_VMEM_LIMIT_BYTES)


def _rms_norm(x, g):
    return x * lax.rsqrt(jnp.mean(x * x, axis=-1, keepdims=True) + EPS) * g


def _dot(a, b):
    return jnp.dot(a, b, preferred_element_type=F32)


def _dot_nt(a, b):
    return lax.dot_general(a, b, (((1,), (1,)), ((), ())), preferred_element_type=F32)


def _dot_tn(a, b):
    return lax.dot_general(a, b, (((0,), (0,)), ((), ())), preferred_element_type=F32)


def _ffn_kernel(x_ref, g_ref, wg_ref, wu_ref, wd_ref, fg_ref, o_ref, h_ref, *, final_norm):
    j = pl.program_id(1)

    @pl.when(j == 0)
    def _():
        h_ref[...] = _rms_norm(x_ref[...], g_ref[...]).astype(BF16)
        o_ref[...] = jnp.zeros_like(o_ref)

    h = h_ref[...]
    gate = _dot(h, wg_ref[...])
    up = _dot(h, wu_ref[...])
    act = (gate * jax.nn.sigmoid(gate) * up).astype(BF16)
    o_ref[...] += _dot(act, wd_ref[...])

    @pl.when(j == pl.num_programs(1) - 1)
    def _():
        y = x_ref[...] + 0.5 * o_ref[...]
        if final_norm:
            y = _rms_norm(y, fg_ref[...])
        o_ref[...] = y


def _ffn_half(x, g, wg, wu, wd, final_g=None):
    m, d = x.shape
    dff = wg.shape[1]
    fg = jnp.ones((1, d), F32) if final_g is None else final_g.reshape(1, d)
    return pl.pallas_call(
        functools.partial(_ffn_kernel, final_norm=final_g is not None),
        grid=(m // ROW_TILE, dff // FF_TILE),
        in_specs=[
            pl.BlockSpec((ROW_TILE, d), lambda i, j: (i, 0)),
            pl.BlockSpec((1, d), lambda i, j: (0, 0)),
            pl.BlockSpec((d, FF_TILE), lambda i, j: (0, j)),
            pl.BlockSpec((d, FF_TILE), lambda i, j: (0, j)),
            pl.BlockSpec((FF_TILE, d), lambda i, j: (j, 0)),
            pl.BlockSpec((1, d), lambda i, j: (0, 0)),
        ],
        out_specs=pl.BlockSpec((ROW_TILE, d), lambda i, j: (i, 0)),
        out_shape=jax.ShapeDtypeStruct((m, d), F32),
        scratch_shapes=[pltpu.VMEM((ROW_TILE, d), BF16)],
        compiler_params=_params("parallel", "arbitrary"),
        name="ffn_half",
    )(x, g.reshape(1, d), wg, wu, wd, fg)


def _gla_in_kernel(x_ref, g_ref, w_ref, wa_ref, wb_ref, bb_ref, proj_ref, lg_ref, h_ref):
    j = pl.program_id(1)

    @pl.when(j == 0)
    def _():
        h = _rms_norm(x_ref[...], g_ref[...]).astype(BF16)
        h_ref[...] = h
        low = _dot(h, wa_ref[...]).astype(BF16)
        z = _dot(low, wb_ref[...]) + bb_ref[...]
        log_sig = jnp.minimum(z, 0.0) - jnp.log(1.0 + jnp.exp(-jnp.abs(z)))
        lg_ref[...] = log_sig / GLA_GATE_NORM

    proj_ref[...] = _dot(h_ref[...], w_ref[...])


def _gla_in(x, g, w_main, w_a, w_b, b_b):
    m, d = x.shape
    n = w_main.shape[1]
    dk = w_b.shape[1]
    return pl.pallas_call(
        _gla_in_kernel,
        grid=(m // ROW_TILE, n // GLA_IN_TILE),
        in_specs=[
            pl.BlockSpec((ROW_TILE, d), lambda i, j: (i, 0)),
            pl.BlockSpec((1, d), lambda i, j: (0, 0)),
            pl.BlockSpec((d, GLA_IN_TILE), lambda i, j: (0, j)),
            pl.BlockSpec((d, V7X_LANES), lambda i, j: (0, 0)),
            pl.BlockSpec((V7X_LANES, dk), lambda i, j: (0, 0)),
            pl.BlockSpec((1, dk), lambda i, j: (0, 0)),
        ],
        out_specs=[
            pl.BlockSpec((ROW_TILE, GLA_IN_TILE), lambda i, j: (i, j)),
            pl.BlockSpec((ROW_TILE, dk), lambda i, j: (i, 0)),
        ],
        out_shape=[
            jax.ShapeDtypeStruct((m, n), F32),
            jax.ShapeDtypeStruct((m, dk), F32),
        ],
        scratch_shapes=[pltpu.VMEM((ROW_TILE, d), BF16)],
        compiler_params=_params("parallel", "arbitrary"),
        name="gla_in",
    )(x, g.reshape(1, d), w_main, w_a, w_b, b_b.reshape(1, dk))


def _gla_core_kernel(q_ref, k_ref, v_ref, r_ref, lg_ref, tri_ref, hn_ref, o_ref, st_ref,
                     *, seq_len, q_scale):
    c = pl.program_id(2)
    chunk, dk = q_ref.shape[1], q_ref.shape[2]

    @pl.when(c == 0)
    def _():
        st_ref[...] = jnp.zeros_like(st_ref)

    row = lax.broadcasted_iota(jnp.int32, (chunk, 1), 0)
    valid = row < (seq_len - c * chunk)
    q = jnp.where(valid, q_ref[0], 0.0) * q_scale
    k = jnp.where(valid, k_ref[0], 0.0)
    v = jnp.where(valid, v_ref[0], 0.0).astype(BF16)
    lg = jnp.where(valid, lg_ref[0], 0.0)

    lg_hi = lg.astype(BF16)
    lg_lo = (lg - lg_hi.astype(F32)).astype(BF16)
    tri = tri_ref[...]
    b = _dot(tri, lg_hi) + _dot(tri, lg_lo)

    ri = lax.broadcasted_iota(jnp.int32, (chunk, chunk), 0)
    ci = lax.broadcasted_iota(jnp.int32, (chunk, chunk), 1)
    a = jnp.where(ri == ci, jnp.sum(q * k, axis=-1, keepdims=True), 0.0)

    n_levels = chunk.bit_length() - 1
    for lev in range(n_levels):
        s = 1 << lev
        if s == 1:
            e = jnp.where((row & 1) == 1, lg, 0.0)
        elif s == 2:
            nxt = pltpu.roll(lg, chunk - 1, 0)
            prv = pltpu.roll(lg, 1, 0)
            pos = row & 3
            e = jnp.where(pos == 0, nxt, jnp.where(pos == 2, lg, jnp.where(pos == 3, lg + prv, 0.0)))
        else:
            b3 = b.reshape(chunk // (2 * s), 2 * s, dk)
            e = (-jnp.abs(b3 - b3[:, s - 1:s, :])).reshape(chunk, dk)
        f = jnp.exp(e)
        is_q = ((row >> lev) & 1) == 1
        qs = jnp.where(is_q, q * f, 0.0).astype(BF16)
        ks = jnp.where(is_q, 0.0, k * f).astype(BF16)
        a_lev = _dot_nt(qs, ks)
        a = a + jnp.where((ri >> (lev + 1)) == (ci >> (lev + 1)), a_lev, 0.0)

    b_last = b[chunk - 1:chunk, :]
    qs = (q * jnp.exp(b)).astype(BF16)
    ks = (k * jnp.exp(b_last - b)).astype(BF16)
    st = st_ref[...]
    o = _dot(a.astype(BF16), v) + _dot_nt(qs, st.astype(BF16))
    st_ref[...] = st * jnp.exp(b_last) + _dot_tn(v, ks)

    o = _rms_norm(o, hn_ref[...])
    r = r_ref[0]
    o_ref[0] = (o * (r * jax.nn.sigmoid(r))).astype(o_ref.dtype)


def _gla_core(proj, lg, head_norm, batch, seq_len):
    dk_all = lg.shape[-1]
    dk = dk_all // GLA_HEADS
    dv_all = (proj.shape[-1] - 2 * dk_all) // 2
    dv = dv_all // GLA_HEADS
    proj = proj.reshape(batch, seq_len, proj.shape[-1])
    lg = lg.reshape(batch, seq_len, dk_all)
    c = GLA_CHUNK
    tri = jnp.asarray(np.tril(np.ones((c, c), np.float32)), BF16)
    k_blk = dk_all // dk
    v_blk = 2 * dk_all // dv
    r_blk = (2 * dk_all + dv_all) // dv
    out = pl.pallas_call(
        functools.partial(_gla_core_kernel, seq_len=seq_len, q_scale=dk ** -0.5),
        grid=(batch, GLA_HEADS, pl.cdiv(seq_len, c)),
        in_specs=[
            pl.BlockSpec((1, c, dk), lambda b, h, i: (b, i, h)),
            pl.BlockSpec((1, c, dk), lambda b, h, i: (b, i, k_blk + h)),
            pl.BlockSpec((1, c, dv), lambda b, h, i: (b, i, v_blk + h)),
            pl.BlockSpec((1, c, dv), lambda b, h, i: (b, i, r_blk + h)),
            pl.BlockSpec((1, c, dk), lambda b, h, i: (b, i, h)),
            pl.BlockSpec((c, c), lambda b, h, i: (0, 0)),
            pl.BlockSpec((1, dv), lambda b, h, i: (0, 0)),
        ],
        out_specs=pl.BlockSpec((1, c, dv), lambda b, h, i: (b, i, h)),
        out_shape=jax.ShapeDtypeStruct((batch, seq_len, dv_all), BF16),
        scratch_shapes=[pltpu.VMEM((dv, dk), F32)],
        compiler_params=_params("parallel", "parallel", "arbitrary"),
        name="gla_core",
    )(proj, proj, proj, proj, lg, tri, head_norm.reshape(1, dv))
    return out.reshape(batch * seq_len, dv_all)


def _proj_add_kernel(x_ref, a_ref, w_ref, o_ref):
    o_ref[...] = x_ref[...] + _dot(a_ref[...], w_ref[...])


def _proj_add(x, a, w):
    m, d = x.shape
    kdim = a.shape[1]
    return pl.pallas_call(
        _proj_add_kernel,
        grid=(m // ROW_TILE,),
        in_specs=[
            pl.BlockSpec((ROW_TILE, d), lambda i: (i, 0)),
            pl.BlockSpec((ROW_TILE, kdim), lambda i: (i, 0)),
            pl.BlockSpec((kdim, d), lambda i: (0, 0)),
        ],
        out_specs=pl.BlockSpec((ROW_TILE, d), lambda i: (i, 0)),
        out_shape=jax.ShapeDtypeStruct((m, d), F32),
        compiler_params=_params("parallel"),
        name="gla_out",
    )(x, a, w)


def _pool_kernel(x_ref, halo_ref, g_ref, w_ref, b_ref, sc_ref, o_ref, *, tiles_per_seq):
    i = pl.program_id(0)
    tile, d = x_ref.shape
    gw = w_ref.shape[1]
    first = (i % tiles_per_seq) == 0
    x = x_ref[...]
    h = _rms_norm(x, g_ref[...])
    h_halo = jnp.where(first, 0.0, _rms_norm(halo_ref[...], g_ref[...]))
    ext = jnp.concatenate([h_halo, h], axis=0)
    t = (i % tiles_per_seq) * tile + lax.broadcasted_iota(jnp.int32, (tile, 1), 0)
    for grp, win in enumerate(POOL_WINDOWS):
        cols = slice(grp * gw, (grp + 1) * gw)
        acc = ext[:, cols]
        step = 1
        while step < win:
            acc = acc + pltpu.roll(acc, step, 0)
            step *= 2
        cnt = jnp.minimum(t + 1, win).astype(F32)
        pooled = acc[POOL_HALO:, :] / cnt - h[:, cols]
        y = _dot(pooled.astype(BF16), w_ref[grp]) + b_ref[grp]
        o_ref[:, cols] = x[:, cols] + y * sc_ref[:, cols]


def _pool_mixer(x, g, w, b, scale, seq_len):
    m, d = x.shape
    groups, gw, _ = w.shape
    tiles_per_seq = seq_len // ROW_TILE
    halo_blocks = ROW_TILE // POOL_HALO
    return pl.pallas_call(
        functools.partial(_pool_kernel, tiles_per_seq=tiles_per_seq),
        grid=(m // ROW_TILE,),
        in_specs=[
            pl.BlockSpec((ROW_TILE, d), lambda i: (i, 0)),
            pl.BlockSpec((POOL_HALO, d), lambda i: (jnp.maximum(i * halo_blocks - 1, 0), 0)),
            pl.BlockSpec((1, d), lambda i: (0, 0)),
            pl.BlockSpec((groups, gw, gw), lambda i: (0, 0, 0)),
            pl.BlockSpec((groups, 1, gw), lambda i: (0, 0, 0)),
            pl.BlockSpec((1, d), lambda i: (0, 0)),
        ],
        out_specs=pl.BlockSpec((ROW_TILE, d), lambda i: (i, 0)),
        out_shape=jax.ShapeDtypeStruct((m, d), F32),
        compiler_params=_params("parallel"),
        name="pool_mixer",
    )(x, x, g.reshape(1, d), w, b.reshape(groups, 1, gw), scale.reshape(1, d))


def kernel(x, meta, ffn_norm, ffn_w_gate, ffn_w_up, ffn_w_down, gla_norm, gla_w_in, gla_w_lr, gla_b_lr,
           gla_head_norm, gla_w_out, pool_norm, pool_w, pool_b, pool_scale, final_norm):
    batch, seq, d = x.shape
    seq_len = seq + N_META
    depth = ffn_norm.shape[0]
    dk_all = gla_w_lr.shape[-1]
    dv_all = gla_w_out.shape[1]
    assert (batch * seq_len) % ROW_TILE == 0 and seq_len % ROW_TILE == 0

    m = jnp.broadcast_to(meta.astype(x.dtype)[None], (batch, N_META, d))
    xs = jnp.concatenate([m, x], axis=1).reshape(batch * seq_len, d)

    wg = ffn_w_gate.astype(BF16)
    wu = ffn_w_up.astype(BF16)
    wd = ffn_w_down.astype(BF16)

    for i in range(depth):
        xs = _ffn_half(xs, ffn_norm[i, 0], wg[i, 0], wu[i, 0], wd[i, 0])
        j = i // 2
        if i % 2 == 0:
            w_in = gla_w_in[j]
            lr0 = 2 * dk_all + dv_all
            w_main = jnp.concatenate([w_in[:, :lr0], w_in[:, lr0 + GLA_GATE_RANK:]], axis=1).astype(BF16)
            w_a = jnp.pad(w_in[:, lr0:lr0 + GLA_GATE_RANK], ((0, 0), (0, V7X_LANES - GLA_GATE_RANK))).astype(BF16)
            w_b = jnp.pad(gla_w_lr[j], ((0, V7X_LANES - GLA_GATE_RANK), (0, 0))).astype(BF16)
            proj, lg = _gla_in(xs, gla_norm[j], w_main, w_a, w_b, gla_b_lr[j])
            gated = _gla_core(proj, lg, gla_head_norm[j], batch, seq_len)
            xs = _proj_add(xs, gated, gla_w_out[j].astype(BF16))
        else:
            xs = _pool_mixer(xs, pool_norm[j], pool_w[j].astype(BF16), pool_b[j], pool_scale[j], seq_len)
        last = i == depth - 1
        xs = _ffn_half(xs, ffn_norm[i, 1], wg[i, 1], wu[i, 1], wd[i, 1], final_norm if last else None)

    return xs.reshape(batch, seq_len, d)[:, N_META:]
```

```python
import functools

import jax
import jax.numpy as jnp
import numpy as np
from jax import lax
from jax.experimental import pallas as pl
from jax.experimental.pallas import tpu as pltpu

F32 = jnp.float32
BF16 = jnp.bfloat16

EPS = 1e-6
N_META = 16
GLA_HEADS = 4
GLA_GATE_RANK = 16
GLA_GATE_NORM = 16.0
POOL_WINDOWS = (2, 4, 8, 16)

V7X_LANES = 128
V7X_VMEM_LIMIT_BYTES = 56 * 1024 * 1024

ROW_TILE = 688
FFN_ROW_TILE = 1376
FF_TILE = 256
GLA_CHUNK = 256
GLA_IN_TILE = 1024
POOL_HALO = 16


def _params(*semantics):
    return pltpu.CompilerParams(dimension_semantics=semantics,
                                vmem_limit_bytes=V7X_VMEM_LIMIT_BYTES)


def _rms_norm(x, g):
    return x * lax.rsqrt(jnp.mean(x * x, axis=-1, keepdims=True) + EPS) * g


def _dot(a, b):
    return jnp.dot(a, b, preferred_element_type=F32)


def _dot_nt(a, b):
    return lax.dot_general(a, b, (((1,), (1,)), ((), ())), preferred_element_type=F32)


def _dot_tn(a, b):
    return lax.dot_general(a, b, (((0,), (0,)), ((), ())), preferred_element_type=F32)


def _ffn_kernel(x_hbm, g_ref, wg_ref, wu_ref, wd_ref, fg_ref, o_hbm, acc_ref, h_ref, in_sem, out_sem,
                *, n_row_tiles, final_norm):
    i = pl.program_id(0)
    j = pl.program_id(1)
    last_j = pl.num_programs(1) - 1
    tile = acc_ref.shape[1]
    slot = i % 2

    def x_copy(t, s):
        return pltpu.make_async_copy(x_hbm.at[pl.ds(t * tile, tile)], acc_ref.at[s], in_sem.at[s])

    def o_copy(t, s):
        return pltpu.make_async_copy(acc_ref.at[s], o_hbm.at[pl.ds(t * tile, tile)], out_sem.at[s])

    @pl.when((i == 0) & (j == 0))
    def _():
        x_copy(0, 0).start()

    @pl.when(j == 0)
    def _():
        x_copy(i, slot).wait()
        x = acc_ref[slot]
        h_ref[...] = _rms_norm(x, g_ref[...]).astype(BF16)
        acc_ref[slot] = 2.0 * x

    @pl.when((j == 1) & (i + 1 < n_row_tiles))
    def _():
        @pl.when(i >= 1)
        def _():
            o_copy(i - 1, 1 - slot).wait()
        x_copy(i + 1, 1 - slot).start()

    h = h_ref[...]
    gate = _dot(h, wg_ref[...].astype(BF16))
    up = _dot(h, wu_ref[...].astype(BF16))
    act = (gate * jax.nn.sigmoid(gate) * up).astype(BF16)
    acc_ref[slot] += _dot(act, wd_ref[...].astype(BF16))

    @pl.when(j == last_j)
    def _():
        y = 0.5 * acc_ref[slot]
        if final_norm:
            y = _rms_norm(y, fg_ref[...])
        acc_ref[slot] = y
        o_copy(i, slot).start()

    @pl.when((i == n_row_tiles - 1) & (j == last_j))
    def _():
        if n_row_tiles > 1:
            o_copy(i - 1, 1 - slot).wait()
        o_copy(i, slot).wait()


def _ffn_half(x, g, w_gate, w_up, w_down, layer, half, final_g=None):
    m, d = x.shape
    dff = w_gate.shape[-1]
    n_row_tiles = m // FFN_ROW_TILE
    assert n_row_tiles * FFN_ROW_TILE == m and dff % FF_TILE == 0 and dff // FF_TILE >= 2
    fg = jnp.ones((1, d), F32) if final_g is None else final_g.reshape(1, d)
    return pl.pallas_call(
        functools.partial(_ffn_kernel, n_row_tiles=n_row_tiles, final_norm=final_g is not None),
        grid=(n_row_tiles, dff // FF_TILE),
        in_specs=[
            pl.BlockSpec(memory_space=pl.ANY),
            pl.BlockSpec((1, d), lambda i, j: (0, 0)),
            pl.BlockSpec((None, None, d, FF_TILE), lambda i, j: (layer, half, 0, j)),
            pl.BlockSpec((None, None, d, FF_TILE), lambda i, j: (layer, half, 0, j)),
            pl.BlockSpec((None, None, FF_TILE, d), lambda i, j: (layer, half, j, 0)),
            pl.BlockSpec((1, d), lambda i, j: (0, 0)),
        ],
        out_specs=pl.BlockSpec(memory_space=pl.ANY),
        out_shape=jax.ShapeDtypeStruct((m, d), F32),
        scratch_shapes=[
            pltpu.VMEM((2, FFN_ROW_TILE, d), F32),
            pltpu.VMEM((FFN_ROW_TILE, d), BF16),
            pltpu.SemaphoreType.DMA((2,)),
            pltpu.SemaphoreType.DMA((2,)),
        ],
        compiler_params=_params("arbitrary", "arbitrary"),
        name="ffn_half",
    )(x, g.reshape(1, d), w_gate, w_up, w_down, fg)


def _gla_in_kernel(x_ref, g_ref, w_ref, wa_ref, wb_ref, bb_ref, proj_ref, lg_ref, h_ref):
    j = pl.program_id(1)

    @pl.when(j == 0)
    def _():
        h = _rms_norm(x_ref[...], g_ref[...]).astype(BF16)
        h_ref[...] = h
        low = _dot(h, wa_ref[...]).astype(BF16)
        z = _dot(low, wb_ref[...]) + bb_ref[...]
        log_sig = jnp.minimum(z, 0.0) - jnp.log(1.0 + jnp.exp(-jnp.abs(z)))
        lg_ref[...] = log_sig / GLA_GATE_NORM

    proj_ref[...] = _dot(h_ref[...], w_ref[...])


def _gla_in(x, g, w_main, w_a, w_b, b_b):
    m, d = x.shape
    n = w_main.shape[1]
    dk = w_b.shape[1]
    return pl.pallas_call(
        _gla_in_kernel,
        grid=(m // ROW_TILE, n // GLA_IN_TILE),
        in_specs=[
            pl.BlockSpec((ROW_TILE, d), lambda i, j: (i, 0)),
            pl.BlockSpec((1, d), lambda i, j: (0, 0)),
            pl.BlockSpec((d, GLA_IN_TILE), lambda i, j: (0, j)),
            pl.BlockSpec((d, V7X_LANES), lambda i, j: (0, 0)),
            pl.BlockSpec((V7X_LANES, dk), lambda i, j: (0, 0)),
            pl.BlockSpec((1, dk), lambda i, j: (0, 0)),
        ],
        out_specs=[
            pl.BlockSpec((ROW_TILE, GLA_IN_TILE), lambda i, j: (i, j)),
            pl.BlockSpec((ROW_TILE, dk), lambda i, j: (i, 0)),
        ],
        out_shape=[
            jax.ShapeDtypeStruct((m, n), F32),
            jax.ShapeDtypeStruct((m, dk), F32),
        ],
        scratch_shapes=[pltpu.VMEM((ROW_TILE, d), BF16)],
        compiler_params=_params("parallel", "arbitrary"),
        name="gla_in",
    )(x, g.reshape(1, d), w_main, w_a, w_b, b_b.reshape(1, dk))


def _gla_level_masks(chunk):
    i = np.arange(chunk)[:, None]
    j = np.arange(chunk)[None, :]
    masks = []
    for lev in range(chunk.bit_length() - 1):
        same_block = (i >> (lev + 1)) == (j >> (lev + 1))
        masks.append(same_block & (((i >> lev) & 1) == 1) & (((j >> lev) & 1) == 0))
    masks.append(i == j)
    return np.stack(masks).astype(np.float32)


def _gla_core_kernel(q_ref, k_ref, v_ref, r_ref, lg_ref, tri_ref, lm_ref, hn_ref, o_ref, st_ref,
                     *, seq_len, q_scale):
    c = pl.program_id(2)
    chunk, dk = q_ref.shape[1], q_ref.shape[2]

    @pl.when(c == 0)
    def _():
        st_ref[...] = jnp.zeros_like(st_ref)

    row = lax.broadcasted_iota(jnp.int32, (chunk, 1), 0)
    valid = row < (seq_len - c * chunk)
    q = jnp.where(valid, q_ref[0], 0.0) * q_scale
    k = jnp.where(valid, k_ref[0], 0.0)
    v = jnp.where(valid, v_ref[0], 0.0).astype(BF16)
    lg = jnp.where(valid, lg_ref[0], 0.0)

    lg_hi = lg.astype(BF16)
    lg_lo = (lg - lg_hi.astype(F32)).astype(BF16)
    tri = tri_ref[...]
    b = _dot(tri, lg_hi) + _dot(tri, lg_lo)

    n_levels = chunk.bit_length() - 1
    a = jnp.sum(q * k, axis=-1, keepdims=True) * lm_ref[n_levels]
    for lev in range(n_levels):
        s = 1 << lev
        if s == 1:
            e = jnp.where((row & 1) == 1, lg, 0.0)
        elif s == 2:
            nxt = pltpu.roll(lg, chunk - 1, 0)
            prv = pltpu.roll(lg, 1, 0)
            pos = row & 3
            e = jnp.where(pos == 0, nxt, jnp.where(pos == 2, lg, jnp.where(pos == 3, lg + prv, 0.0)))
        else:
            b3 = b.reshape(chunk // (2 * s), 2 * s, dk)
            e = (-jnp.abs(b3 - b3[:, s - 1:s, :])).reshape(chunk, dk)
        f = jnp.exp(e)
        a = a + _dot_nt((q * f).astype(BF16), (k * f).astype(BF16)) * lm_ref[lev]

    b_last = b[chunk - 1:chunk, :]
    qs = (q * jnp.exp(b)).astype(BF16)
    ks = (k * jnp.exp(b_last - b)).astype(BF16)
    st = st_ref[...]
    o = _dot(a.astype(BF16), v) + _dot_nt(qs, st.astype(BF16))
    st_ref[...] = st * jnp.exp(b_last) + _dot_tn(v, ks)

    o = _rms_norm(o, hn_ref[...])
    r = r_ref[0]
    o_ref[0] = (o * (r * jax.nn.sigmoid(r))).astype(o_ref.dtype)


def _gla_core(proj, lg, head_norm, batch, seq_len):
    dk_all = lg.shape[-1]
    dk = dk_all // GLA_HEADS
    dv_all = (proj.shape[-1] - 2 * dk_all) // 2
    dv = dv_all // GLA_HEADS
    proj = proj.reshape(batch, seq_len, proj.shape[-1])
    lg = lg.reshape(batch, seq_len, dk_all)
    c = GLA_CHUNK
    tri = jnp.asarray(np.tril(np.ones((c, c), np.float32)), BF16)
    level_masks = jnp.asarray(_gla_level_masks(c))
    k_blk = dk_all // dk
    v_blk = 2 * dk_all // dv
    r_blk = (2 * dk_all + dv_all) // dv
    out = pl.pallas_call(
        functools.partial(_gla_core_kernel, seq_len=seq_len, q_scale=dk ** -0.5),
        grid=(batch, GLA_HEADS, pl.cdiv(seq_len, c)),
        in_specs=[
            pl.BlockSpec((1, c, dk), lambda b, h, i: (b, i, h)),
            pl.BlockSpec((1, c, dk), lambda b, h, i: (b, i, k_blk + h)),
            pl.BlockSpec((1, c, dv), lambda b, h, i: (b, i, v_blk + h)),
            pl.BlockSpec((1, c, dv), lambda b, h, i: (b, i, r_blk + h)),
            pl.BlockSpec((1, c, dk), lambda b, h, i: (b, i, h)),
            pl.BlockSpec((c, c), lambda b, h, i: (0, 0)),
            pl.BlockSpec(level_masks.shape, lambda b, h, i: (0, 0, 0)),
            pl.BlockSpec((1, dv), lambda b, h, i: (0, 0)),
        ],
        out_specs=pl.BlockSpec((1, c, dv), lambda b, h, i: (b, i, h)),
        out_shape=jax.ShapeDtypeStruct((batch, seq_len, dv_all), BF16),
        scratch_shapes=[pltpu.VMEM((dv, dk), F32)],
        compiler_params=_params("parallel", "parallel", "arbitrary"),
        name="gla_core",
    )(proj, proj, proj, proj, lg, tri, level_masks, head_norm.reshape(1, dv))
    return out.reshape(batch * seq_len, dv_all)


def _proj_add_kernel(x_ref, a_ref, w_ref, o_ref):
    o_ref[...] = x_ref[...] + _dot(a_ref[...], w_ref[...])


def _proj_add(x, a, w):
    m, d = x.shape
    kdim = a.shape[1]
    return pl.pallas_call(
        _proj_add_kernel,
        grid=(m // ROW_TILE,),
        in_specs=[
            pl.BlockSpec((ROW_TILE, d), lambda i: (i, 0)),
            pl.BlockSpec((ROW_TILE, kdim), lambda i: (i, 0)),
            pl.BlockSpec((kdim, d), lambda i: (0, 0)),
        ],
        out_specs=pl.BlockSpec((ROW_TILE, d), lambda i: (i, 0)),
        out_shape=jax.ShapeDtypeStruct((m, d), F32),
        compiler_params=_params("parallel"),
        name="gla_out",
    )(x, a, w)


def _pool_kernel(x_ref, halo_ref, g_ref, w_ref, b_ref, sc_ref, o_ref, *, tiles_per_seq):
    i = pl.program_id(0)
    tile, d = x_ref.shape
    gw = w_ref.shape[1]
    first = (i % tiles_per_seq) == 0
    x = x_ref[...]
    h = _rms_norm(x, g_ref[...])
    h_halo = jnp.where(first, 0.0, _rms_norm(halo_ref[...], g_ref[...]))
    ext = jnp.concatenate([h_halo, h], axis=0)
    t = (i % tiles_per_seq) * tile + lax.broadcasted_iota(jnp.int32, (tile, 1), 0)
    for grp, win in enumerate(POOL_WINDOWS):
        cols = slice(grp * gw, (grp + 1) * gw)
        acc = ext[:, cols]
        step = 1
        while step < win:
            acc = acc + pltpu.roll(acc, step, 0)
            step *= 2
        cnt = jnp.minimum(t + 1, win).astype(F32)
        pooled = acc[POOL_HALO:, :] / cnt - h[:, cols]
        y = _dot(pooled.astype(BF16), w_ref[grp]) + b_ref[grp]
        o_ref[:, cols] = x[:, cols] + y * sc_ref[:, cols]


def _pool_mixer(x, g, w, b, scale, seq_len):
    m, d = x.shape
    groups, gw, _ = w.shape
    tiles_per_seq = seq_len // ROW_TILE
    halo_blocks = ROW_TILE // POOL_HALO
    return pl.pallas_call(
        functools.partial(_pool_kernel, tiles_per_seq=tiles_per_seq),
        grid=(m // ROW_TILE,),
        in_specs=[
            pl.BlockSpec((ROW_TILE, d), lambda i: (i, 0)),
            pl.BlockSpec((POOL_HALO, d), lambda i: (jnp.maximum(i * halo_blocks - 1, 0), 0)),
            pl.BlockSpec((1, d), lambda i: (0, 0)),
            pl.BlockSpec((groups, gw, gw), lambda i: (0, 0, 0)),
            pl.BlockSpec((groups, 1, gw), lambda i: (0, 0, 0)),
            pl.BlockSpec((1, d), lambda i: (0, 0)),
        ],
        out_specs=pl.BlockSpec((ROW_TILE, d), lambda i: (i, 0)),
        out_shape=jax.ShapeDtypeStruct((m, d), F32),
        compiler_params=_params("parallel"),
        name="pool_mixer",
    )(x, x, g.reshape(1, d), w, b.reshape(groups, 1, gw), scale.reshape(1, d))


def kernel(x, meta, ffn_norm, ffn_w_gate, ffn_w_up, ffn_w_down, gla_norm, gla_w_in, gla_w_lr, gla_b_lr,
           gla_head_norm, gla_w_out, pool_norm, pool_w, pool_b, pool_scale, final_norm):
    batch, seq, d = x.shape
    seq_len = seq + N_META
    depth = ffn_norm.shape[0]
    dk_all = gla_w_lr.shape[-1]
    dv_all = gla_w_out.shape[1]
    assert (batch * seq_len) % ROW_TILE == 0 and seq_len % ROW_TILE == 0

    m = jnp.broadcast_to(meta.astype(x.dtype)[None], (batch, N_META, d))
    xs = jnp.concatenate([m, x], axis=1).reshape(batch * seq_len, d)

    for i in range(depth):
        xs = _ffn_half(xs, ffn_norm[i, 0], ffn_w_gate, ffn_w_up, ffn_w_down, i, 0)
        j = i // 2
        if i % 2 == 0:
            w_in = gla_w_in[j]
            lr0 = 2 * dk_all + dv_all
            w_main = jnp.concatenate([w_in[:, :lr0].astype(BF16), w_in[:, lr0 + GLA_GATE_RANK:].astype(BF16)], axis=1)
            w_a = jnp.pad(w_in[:, lr0:lr0 + GLA_GATE_RANK], ((0, 0), (0, V7X_LANES - GLA_GATE_RANK))).astype(BF16)
            w_b = jnp.pad(gla_w_lr[j], ((0, V7X_LANES - GLA_GATE_RANK), (0, 0))).astype(BF16)
            proj, lg = _gla_in(xs, gla_norm[j], w_main, w_a, w_b, gla_b_lr[j])
            gated = _gla_core(proj, lg, gla_head_norm[j], batch, seq_len)
            xs = _proj_add(xs, gated, gla_w_out[j].astype(BF16))
        else:
            xs = _pool_mixer(xs, pool_norm[j], pool_w[j].astype(BF16), pool_b[j], pool_scale[j], seq_len)
        last = i == depth - 1
        xs = _ffn_half(xs, ffn_norm[i, 1], ffn_w_gate, ffn_w_up, ffn_w_down, i, 1, final_norm if last else None)

    return xs.reshape(batch, seq_len, d)[:, N_META:]
```

```python
import functools
import math

import jax
import jax.numpy as jnp
import numpy as np
from jax import lax
from jax.experimental import pallas as pl
from jax.experimental.pallas import tpu as pltpu

F32 = jnp.float32
BF16 = jnp.bfloat16

EPS = 1e-6
N_META = 16
GLA_HEADS = 4
GLA_GATE_RANK = 16
GLA_GATE_NORM = 16.0
POOL_WINDOWS = (2, 4, 8, 16)
LOG2E = math.log2(math.e)

V7X_LANES = 128
V7X_BF16_SUBLANES = 16
V7X_VMEM_LIMIT_BYTES = 56 * 1024 * 1024

ROW_TILE = 688
FFN_ROW_TILE = 1376
FF_TILE = 256
GLA_CHUNK = 256
GLA_IN_TILE = 512
POOL_HALO = 16
ROW_GROUP_UNROLL = 8


def _params(*semantics):
    return pltpu.CompilerParams(dimension_semantics=semantics,
                                vmem_limit_bytes=V7X_VMEM_LIMIT_BYTES)


def _rms_norm(x, g):
    return x * lax.rsqrt(jnp.mean(x * x, axis=-1, keepdims=True) + EPS) * g


def _dot(a, b):
    return jnp.dot(a, b, preferred_element_type=F32)


def _dot_nt(a, b):
    return lax.dot_general(a, b, (((1,), (1,)), ((), ())), preferred_element_type=F32)


def _dot_tn(a, b):
    return lax.dot_general(a, b, (((0,), (0,)), ((), ())), preferred_element_type=F32)


def _row_pieces(tile, seq_len, n_meta):
    period = math.lcm(tile, seq_len)
    phases = []
    for p in range(period // tile):
        pieces, r, end = [], p * tile, (p + 1) * tile
        while r < end:
            b, t = divmod(r, seq_len)
            if t < n_meta:
                n = min(n_meta - t, end - r)
                pieces.append(("meta", b, t, r - p * tile, n))
            else:
                n = min(seq_len - t, end - r)
                pieces.append(("seq", b, t - n_meta, r - p * tile, n))
            r += n
        phases.append(pieces)
    return phases, period // seq_len


def _tile_dma(action, to_vmem, t, slot, acc_ref, sems, main_hbm, meta_hbm, layout):
    tile = acc_ref.shape[1]

    def run(hbm, vmem, sem):
        copy = pltpu.make_async_copy(hbm, vmem, sem) if to_vmem else pltpu.make_async_copy(vmem, hbm, sem)
        if action == "start":
            copy.start()
        else:
            copy.wait()

    if layout is None:
        run(main_hbm.at[pl.ds(t * tile, tile)], acc_ref.at[slot], sems.at[slot, 0])
        return
    phases, batches = layout
    for p, pieces in enumerate(phases):
        @pl.when(t % len(phases) == p)
        def _():
            base = (t // len(phases)) * batches
            for idx, (kind, b, src, dst, n) in enumerate(pieces):
                vmem = acc_ref.at[slot, pl.ds(dst, n)]
                if kind == "seq":
                    run(main_hbm.at[base + b, pl.ds(src, n)], vmem, sems.at[slot, idx])
                elif to_vmem:
                    run(meta_hbm.at[pl.ds(src, n)], vmem, sems.at[slot, idx])


def _ffn_kernel(*refs, n_row_tiles, in_layout, out_layout, final_norm, emit_h):
    if emit_h:
        (x_hbm, meta_hbm, g_ref, wg_ref, wu_ref, wd_ref, g2_ref, o_hbm, hn_ref,
         acc_ref, h_ref, in_sem, out_sem) = refs
    else:
        (x_hbm, meta_hbm, g_ref, wg_ref, wu_ref, wd_ref, g2_ref, o_hbm,
         acc_ref, h_ref, in_sem, out_sem) = refs
    i = pl.program_id(0)
    j = pl.program_id(1)
    last_j = pl.num_programs(1) - 1
    tile = acc_ref.shape[1]
    group = V7X_BF16_SUBLANES
    slot = i % 2
    load = functools.partial(_tile_dma, to_vmem=True, acc_ref=acc_ref, sems=in_sem, main_hbm=x_hbm,
                             meta_hbm=meta_hbm, layout=in_layout)
    store = functools.partial(_tile_dma, to_vmem=False, acc_ref=acc_ref, sems=out_sem, main_hbm=o_hbm,
                              meta_hbm=None, layout=out_layout)

    def group_rows(r):
        return pl.ds(pl.multiple_of(r * group, group), group)

    @pl.when((i == 0) & (j == 0))
    def _():
        load("start", t=0, slot=0)

    @pl.when(j == 0)
    def _():
        load("wait", t=i, slot=slot)

        def rows_in(r, carry):
            rows = group_rows(r)
            x = acc_ref[slot, rows, :]
            h_ref[rows, :] = _rms_norm(x, g_ref[...]).astype(BF16)
            acc_ref[slot, rows, :] = 2.0 * x
            return carry

        lax.fori_loop(0, tile // group, rows_in, 0, unroll=ROW_GROUP_UNROLL)

    @pl.when((j == 1) & (i + 1 < n_row_tiles))
    def _():
        @pl.when(i >= 1)
        def _():
            store("wait", t=i - 1, slot=1 - slot)
        load("start", t=i + 1, slot=1 - slot)

    h = h_ref[...]
    gate = _dot(h, wg_ref[...].astype(BF16))
    up = _dot(h, wu_ref[...].astype(BF16))
    act = (gate * jax.nn.sigmoid(gate) * up).astype(BF16)
    acc_ref[slot] += _dot(act, wd_ref[...].astype(BF16))

    @pl.when(j == last_j)
    def _():
        def rows_out(r, carry):
            rows = group_rows(r)
            y = 0.5 * acc_ref[slot, rows, :]
            if final_norm:
                y = _rms_norm(y, g2_ref[...])
            acc_ref[slot, rows, :] = y
            if emit_h:
                hn_ref[rows, :] = _rms_norm(y, g2_ref[...]).astype(BF16)
            return carry

        lax.fori_loop(0, tile // group, rows_out, 0, unroll=ROW_GROUP_UNROLL)
        store("start", t=i, slot=slot)

    @pl.when((i == n_row_tiles - 1) & (j == last_j))
    def _():
        if n_row_tiles > 1:
            store("wait", t=i - 1, slot=1 - slot)
        store("wait", t=i, slot=slot)


def _ffn_half(x, meta, g, w_gate, w_up, w_down, layer, half, *, rows, seq_len, assemble=False, strip=False,
              final_g=None, next_g=None):
    d = x.shape[-1]
    dff = w_gate.shape[-1]
    n_row_tiles = rows // FFN_ROW_TILE
    assert n_row_tiles * FFN_ROW_TILE == rows and dff % FF_TILE == 0 and dff // FF_TILE >= 2
    assert FFN_ROW_TILE % V7X_BF16_SUBLANES == 0 and not (final_g is not None and next_g is not None)
    layout = _row_pieces(FFN_ROW_TILE, seq_len, N_META)
    n_sems = max(len(p) for p in layout[0])
    g2 = final_g if final_g is not None else next_g
    g2 = jnp.ones((1, d), F32) if g2 is None else g2.reshape(1, d)
    batch = rows // seq_len
    out_shape = [jax.ShapeDtypeStruct((batch, seq_len - N_META, d) if strip else (rows, d), F32)]
    out_specs = [pl.BlockSpec(memory_space=pl.ANY)]
    if next_g is not None:
        out_shape.append(jax.ShapeDtypeStruct((rows, d), BF16))
        out_specs.append(pl.BlockSpec((FFN_ROW_TILE, d), lambda i, j: (i, 0)))
    out = pl.pallas_call(
        functools.partial(_ffn_kernel, n_row_tiles=n_row_tiles, in_layout=layout if assemble else None,
                          out_layout=layout if strip else None, final_norm=final_g is not None,
                          emit_h=next_g is not None),
        grid=(n_row_tiles, dff // FF_TILE),
        in_specs=[
            pl.BlockSpec(memory_space=pl.ANY),
            pl.BlockSpec(memory_space=pl.ANY),
            pl.BlockSpec((1, d), lambda i, j: (0, 0)),
            pl.BlockSpec((None, None, d, FF_TILE), lambda i, j: (layer, half, 0, j)),
            pl.BlockSpec((None, None, d, FF_TILE), lambda i, j: (layer, half, 0, j)),
            pl.BlockSpec((None, None, FF_TILE, d), lambda i, j: (layer, half, j, 0)),
            pl.BlockSpec((1, d), lambda i, j: (0, 0)),
        ],
        out_specs=out_specs,
        out_shape=out_shape,
        scratch_shapes=[
            pltpu.VMEM((2, FFN_ROW_TILE, d), F32),
            pltpu.VMEM((FFN_ROW_TILE, d), BF16),
            pltpu.SemaphoreType.DMA((2, n_sems)),
            pltpu.SemaphoreType.DMA((2, n_sems)),
        ],
        compiler_params=_params("arbitrary", "arbitrary"),
        name="ffn_half",
    )(x, meta, g.reshape(1, d), w_gate, w_up, w_down, g2)
    return out if next_g is not None else out[0]


def _gla_in_kernel(h_ref, wqkv_ref, wr_ref, wa_ref, wb_ref, bb_ref, proj_ref, lg_ref, *, n_qkv_blocks):
    j = pl.program_id(1)

    @pl.when(j == 0)
    def _():
        low = _dot(h_ref[...], wa_ref[...]).astype(BF16)
        z = _dot(low, wb_ref[...]) + bb_ref[...]
        log_sig = jnp.minimum(z, 0.0) - jnp.log(1.0 + jnp.exp(-jnp.abs(z)))
        lg_ref[...] = log_sig / GLA_GATE_NORM

    @pl.when(j < n_qkv_blocks)
    def _():
        proj_ref[...] = _dot(h_ref[...], wqkv_ref[...].astype(BF16)).astype(proj_ref.dtype)

    @pl.when(j >= n_qkv_blocks)
    def _():
        proj_ref[...] = _dot(h_ref[...], wr_ref[...]).astype(proj_ref.dtype)


def _gla_in(h, w_in, layer, w_r, w_a, w_b, b_b, qkv_cols):
    m, d = h.shape
    dk = w_b.shape[1]
    n_qkv = qkv_cols // GLA_IN_TILE
    n_r = w_r.shape[1] // GLA_IN_TILE
    assert n_qkv * GLA_IN_TILE == qkv_cols and n_r * GLA_IN_TILE == w_r.shape[1] and m % FFN_ROW_TILE == 0
    return pl.pallas_call(
        functools.partial(_gla_in_kernel, n_qkv_blocks=n_qkv),
        grid=(m // FFN_ROW_TILE, n_qkv + n_r),
        in_specs=[
            pl.BlockSpec((FFN_ROW_TILE, d), lambda i, j: (i, 0)),
            pl.BlockSpec((None, d, GLA_IN_TILE), lambda i, j: (layer, 0, jnp.minimum(j, n_qkv - 1))),
            pl.BlockSpec((d, GLA_IN_TILE), lambda i, j: (0, jnp.maximum(j - n_qkv, 0))),
            pl.BlockSpec((d, V7X_LANES), lambda i, j: (0, 0)),
            pl.BlockSpec((V7X_LANES, dk), lambda i, j: (0, 0)),
            pl.BlockSpec((1, dk), lambda i, j: (0, 0)),
        ],
        out_specs=[
            pl.BlockSpec((FFN_ROW_TILE, GLA_IN_TILE), lambda i, j: (i, j)),
            pl.BlockSpec((FFN_ROW_TILE, dk), lambda i, j: (i, 0)),
        ],
        out_shape=[
            jax.ShapeDtypeStruct((m, qkv_cols + w_r.shape[1]), BF16),
            jax.ShapeDtypeStruct((m, dk), F32),
        ],
        compiler_params=_params("parallel", "arbitrary"),
        name="gla_in",
    )(h, w_in, w_r, w_a, w_b, b_b.reshape(1, dk))


def _gla_level_masks(chunk):
    i = np.arange(chunk)[:, None]
    j = np.arange(chunk)[None, :]
    masks = []
    for lev in range(chunk.bit_length() - 1):
        same_block = (i >> (lev + 1)) == (j >> (lev + 1))
        masks.append(same_block & (((i >> lev) & 1) == 1) & (((j >> lev) & 1) == 0))
    masks.append(i == j)
    return np.stack(masks).astype(np.float32)


def _gla_core_kernel(q_ref, k_ref, v_ref, r_ref, lg_ref, tri_ref, lm_ref, hn_ref, o_ref, st_ref,
                     *, seq_len, q_scale):
    c = pl.program_id(2)
    chunk, dk = q_ref.shape[1], q_ref.shape[2]

    @pl.when(c == 0)
    def _():
        st_ref[...] = jnp.zeros_like(st_ref)

    row = lax.broadcasted_iota(jnp.int32, (chunk, 1), 0)
    valid = row < (seq_len - c * chunk)
    q32 = jnp.where(valid, q_ref[0].astype(F32), 0.0)
    k32 = jnp.where(valid, k_ref[0].astype(F32), 0.0)
    q = q32.astype(BF16)
    k = k32.astype(BF16)
    v = jnp.where(valid, v_ref[0].astype(F32), 0.0).astype(BF16)
    lg = jnp.where(valid, lg_ref[0], 0.0) * LOG2E

    lg_hi = lg.astype(BF16)
    lg_lo = (lg - lg_hi.astype(F32)).astype(BF16)
    tri = tri_ref[...]
    b = _dot(tri, lg_hi) + _dot(tri, lg_lo)

    n_levels = chunk.bit_length() - 1
    a = jnp.sum(q32 * k32, axis=-1, keepdims=True) * lm_ref[n_levels]
    for lev in range(n_levels):
        s = 1 << lev
        if s == 1:
            e = jnp.where((row & 1) == 1, lg, 0.0)
        elif s == 2:
            nxt = pltpu.roll(lg, chunk - 1, 0)
            prv = pltpu.roll(lg, 1, 0)
            pos = row & 3
            e = jnp.where(pos == 0, nxt, jnp.where(pos == 2, lg, jnp.where(pos == 3, lg + prv, 0.0)))
        else:
            b3 = b.reshape(chunk // (2 * s), 2 * s, dk)
            e = (-jnp.abs(b3 - b3[:, s - 1:s, :])).reshape(chunk, dk)
        f = jnp.exp2(e).astype(BF16)
        a = a + _dot_nt(q * f, k * f) * lm_ref[lev]

    b_last = b[chunk - 1:chunk, :]
    qs = q * jnp.exp2(b).astype(BF16)
    ks = k * jnp.exp2(b_last - b).astype(BF16)
    st = st_ref[...]
    o = (_dot(a.astype(BF16), v) + _dot_nt(qs, st.astype(BF16))) * q_scale
    st_ref[...] = st * jnp.exp2(b_last) + _dot_tn(v, ks)

    o = _rms_norm(o, hn_ref[...])
    r = r_ref[0].astype(F32)
    o_ref[0] = (o * (r * jax.nn.sigmoid(r))).astype(o_ref.dtype)


def _gla_core(proj, lg, head_norm, batch, seq_len):
    dk_all = lg.shape[-1]
    dk = dk_all // GLA_HEADS
    dv_all = (proj.shape[-1] - 2 * dk_all) // 2
    dv = dv_all // GLA_HEADS
    proj = proj.reshape(batch, seq_len, proj.shape[-1])
    lg = lg.reshape(batch, seq_len, dk_all)
    c = GLA_CHUNK
    tri = jnp.asarray(np.tril(np.ones((c, c), np.float32)), BF16)
    level_masks = jnp.asarray(_gla_level_masks(c))
    k_blk = dk_all // dk
    v_blk = 2 * dk_all // dv
    r_blk = (2 * dk_all + dv_all) // dv
    out = pl.pallas_call(
        functools.partial(_gla_core_kernel, seq_len=seq_len, q_scale=dk ** -0.5),
        grid=(batch, GLA_HEADS, pl.cdiv(seq_len, c)),
        in_specs=[
            pl.BlockSpec((1, c, dk), lambda b, h, i: (b, i, h)),
            pl.BlockSpec((1, c, dk), lambda b, h, i: (b, i, k_blk + h)),
            pl.BlockSpec((1, c, dv), lambda b, h, i: (b, i, v_blk + h)),
            pl.BlockSpec((1, c, dv), lambda b, h, i: (b, i, r_blk + h)),
            pl.BlockSpec((1, c, dk), lambda b, h, i: (b, i, h)),
            pl.BlockSpec((c, c), lambda b, h, i: (0, 0)),
            pl.BlockSpec(level_masks.shape, lambda b, h, i: (0, 0, 0)),
            pl.BlockSpec((1, dv), lambda b, h, i: (0, 0)),
        ],
        out_specs=pl.BlockSpec((1, c, dv), lambda b, h, i: (b, i, h)),
        out_shape=jax.ShapeDtypeStruct((batch, seq_len, dv_all), BF16),
        scratch_shapes=[pltpu.VMEM((dv, dk), F32)],
        compiler_params=_params("parallel", "parallel", "arbitrary"),
        name="gla_core",
    )(proj, proj, proj, proj, lg, tri, level_masks, head_norm.reshape(1, dv))
    return out.reshape(batch * seq_len, dv_all)


def _proj_add_kernel(x_ref, a_ref, w_ref, o_ref):
    o_ref[...] = x_ref[...] + _dot(a_ref[...], w_ref[...])


def _proj_add(x, a, w):
    m, d = x.shape
    kdim = a.shape[1]
    return pl.pallas_call(
        _proj_add_kernel,
        grid=(m // ROW_TILE,),
        in_specs=[
            pl.BlockSpec((ROW_TILE, d), lambda i: (i, 0)),
            pl.BlockSpec((ROW_TILE, kdim), lambda i: (i, 0)),
            pl.BlockSpec((kdim, d), lambda i: (0, 0)),
        ],
        out_specs=pl.BlockSpec((ROW_TILE, d), lambda i: (i, 0)),
        out_shape=jax.ShapeDtypeStruct((m, d), F32),
        compiler_params=_params("parallel"),
        name="gla_out",
    )(x, a, w)


def _pool_kernel(x_ref, halo_ref, g_ref, w_ref, b_ref, sc_ref, o_ref, *, tiles_per_seq):
    i = pl.program_id(0)
    tile, d = x_ref.shape
    gw = w_ref.shape[1]
    first = (i % tiles_per_seq) == 0
    x = x_ref[...]
    h = _rms_norm(x, g_ref[...])
    h_halo = jnp.where(first, 0.0, _rms_norm(halo_ref[...], g_ref[...]))
    ext = jnp.concatenate([h_halo, h], axis=0)
    t = (i % tiles_per_seq) * tile + lax.broadcasted_iota(jnp.int32, (tile, 1), 0)
    for grp, win in enumerate(POOL_WINDOWS):
        cols = slice(grp * gw, (grp + 1) * gw)
        acc = ext[:, cols]
        step = 1
        while step < win:
            acc = acc + pltpu.roll(acc, step, 0)
            step *= 2
        cnt = jnp.minimum(t + 1, win).astype(F32)
        pooled = acc[POOL_HALO:, :] / cnt - h[:, cols]
        y = _dot(pooled.astype(BF16), w_ref[grp]) + b_ref[grp]
        o_ref[:, cols] = x[:, cols] + y * sc_ref[:, cols]


def _pool_mixer(x, g, w, b, scale, seq_len):
    m, d = x.shape
    groups, gw, _ = w.shape
    tiles_per_seq = seq_len // ROW_TILE
    halo_blocks = ROW_TILE // POOL_HALO
    return pl.pallas_call(
        functools.partial(_pool_kernel, tiles_per_seq=tiles_per_seq),
        grid=(m // ROW_TILE,),
        in_specs=[
            pl.BlockSpec((ROW_TILE, d), lambda i: (i, 0)),
            pl.BlockSpec((POOL_HALO, d), lambda i: (jnp.maximum(i * halo_blocks - 1, 0), 0)),
            pl.BlockSpec((1, d), lambda i: (0, 0)),
            pl.BlockSpec((groups, gw, gw), lambda i: (0, 0, 0)),
            pl.BlockSpec((groups, 1, gw), lambda i: (0, 0, 0)),
            pl.BlockSpec((1, d), lambda i: (0, 0)),
        ],
        out_specs=pl.BlockSpec((ROW_TILE, d), lambda i: (i, 0)),
        out_shape=jax.ShapeDtypeStruct((m, d), F32),
        compiler_params=_params("parallel"),
        name="pool_mixer",
    )(x, x, g.reshape(1, d), w, b.reshape(groups, 1, gw), scale.reshape(1, d))


def kernel(x, meta, ffn_norm, ffn_w_gate, ffn_w_up, ffn_w_down, gla_norm, gla_w_in, gla_w_lr, gla_b_lr,
           gla_head_norm, gla_w_out, pool_norm, pool_w, pool_b, pool_scale, final_norm):
    batch, seq, d = x.shape
    seq_len = seq + N_META
    rows = batch * seq_len
    depth = ffn_norm.shape[0]
    dk_all = gla_w_lr.shape[-1]
    dv_all = gla_w_out.shape[1]
    assert rows % ROW_TILE == 0 and seq_len % ROW_TILE == 0
    meta = meta.astype(x.dtype)
    ffn = functools.partial(_ffn_half, meta=meta, w_gate=ffn_w_gate, w_up=ffn_w_up, w_down=ffn_w_down,
                            rows=rows, seq_len=seq_len)

    xs = x
    for i in range(depth):
        j = i // 2
        gla_layer = i % 2 == 0
        first, last = i == 0, i == depth - 1
        out = ffn(xs, g=ffn_norm[i, 0], layer=i, half=0, assemble=first, next_g=gla_norm[j] if gla_layer else None)
        if gla_layer:
            xs, hn = out
            lr0 = 2 * dk_all + dv_all
            w_r = gla_w_in[j, :, lr0 + GLA_GATE_RANK:].astype(BF16)
            w_a = jnp.pad(gla_w_in[j, :, lr0:lr0 + GLA_GATE_RANK],
                          ((0, 0), (0, V7X_LANES - GLA_GATE_RANK))).astype(BF16)
            w_b = jnp.pad(gla_w_lr[j], ((0, V7X_LANES - GLA_GATE_RANK), (0, 0))).astype(BF16)
            proj, lg = _gla_in(hn, gla_w_in, j, w_r, w_a, w_b, gla_b_lr[j], lr0)
            gated = _gla_core(proj, lg, gla_head_norm[j], batch, seq_len)
            xs = _proj_add(xs, gated, gla_w_out[j].astype(BF16))
        else:
            xs = _pool_mixer(out, pool_norm[j], pool_w[j].astype(BF16), pool_b[j], pool_scale[j], seq_len)
        xs = ffn(xs, g=ffn_norm[i, 1], layer=i, half=1, strip=last, final_g=final_norm if last else None)
    return xs
```

```python
import functools
import math

import jax
import jax.numpy as jnp
import numpy as np
from jax import lax
from jax.experimental import pallas as pl
from jax.experimental.pallas import tpu as pltpu

F32 = jnp.float32
BF16 = jnp.bfloat16

EPS = 1e-6
N_META = 16
GLA_HEADS = 4
GLA_GATE_RANK = 16
GLA_GATE_NORM = 16.0
POOL_WINDOWS = (2, 4, 8, 16)
LOG2E = math.log2(math.e)

V7X_LANES = 128
V7X_BF16_SUBLANES = 16
V7X_VMEM_LIMIT_BYTES = 56 * 1024 * 1024

ROW_TILE = 688
FFN_ROW_TILE = 1376
FF_TILE = 256
GLA_CHUNK = 256
GLA_IN_TILE = 1024
GLA_LG_TILE = 256
POOL_HALO = 16
ROW_GROUP_UNROLL = 8


def _params(*semantics):
    return pltpu.CompilerParams(dimension_semantics=semantics,
                                vmem_limit_bytes=V7X_VMEM_LIMIT_BYTES)


def _rms_norm(x, g):
    return x * lax.rsqrt(jnp.mean(x * x, axis=-1, keepdims=True) + EPS) * g


def _dot(a, b):
    return jnp.dot(a, b, preferred_element_type=F32)


def _dot_nt(a, b):
    return lax.dot_general(a, b, (((1,), (1,)), ((), ())), preferred_element_type=F32)


def _dot_tn(a, b):
    return lax.dot_general(a, b, (((0,), (0,)), ((), ())), preferred_element_type=F32)


def _row_pieces(tile, seq_len, n_meta):
    period = math.lcm(tile, seq_len)
    phases = []
    for p in range(period // tile):
        pieces, r, end = [], p * tile, (p + 1) * tile
        while r < end:
            b, t = divmod(r, seq_len)
            if t < n_meta:
                n = min(n_meta - t, end - r)
                pieces.append(("meta", b, t, r - p * tile, n))
            else:
                n = min(seq_len - t, end - r)
                pieces.append(("seq", b, t - n_meta, r - p * tile, n))
            r += n
        phases.append(pieces)
    return phases, period // seq_len


def _tile_dma(action, to_vmem, t, slot, acc_ref, sems, main_hbm, meta_hbm, layout):
    tile = acc_ref.shape[1]

    def run(hbm, vmem, sem):
        copy = pltpu.make_async_copy(hbm, vmem, sem) if to_vmem else pltpu.make_async_copy(vmem, hbm, sem)
        if action == "start":
            copy.start()
        else:
            copy.wait()

    if layout is None:
        run(main_hbm.at[pl.ds(t * tile, tile)], acc_ref.at[slot], sems.at[slot, 0])
        return
    phases, batches = layout
    for p, pieces in enumerate(phases):
        @pl.when(t % len(phases) == p)
        def _():
            base = (t // len(phases)) * batches
            for idx, (kind, b, src, dst, n) in enumerate(pieces):
                vmem = acc_ref.at[slot, pl.ds(dst, n)]
                if kind == "seq":
                    run(main_hbm.at[base + b, pl.ds(src, n)], vmem, sems.at[slot, idx])
                elif to_vmem:
                    run(meta_hbm.at[pl.ds(src, n)], vmem, sems.at[slot, idx])


def _ffn_kernel(*refs, n_row_tiles, in_layout, out_layout, final_norm, emit_h):
    if emit_h:
        (x_hbm, meta_hbm, g_ref, wg_ref, wu_ref, wd_ref, g2_ref, o_hbm, hn_ref,
         acc_ref, h_ref, in_sem, out_sem) = refs
    else:
        (x_hbm, meta_hbm, g_ref, wg_ref, wu_ref, wd_ref, g2_ref, o_hbm,
         acc_ref, h_ref, in_sem, out_sem) = refs
    i = pl.program_id(0)
    j = pl.program_id(1)
    last_j = pl.num_programs(1) - 1
    tile = acc_ref.shape[1]
    group = V7X_BF16_SUBLANES
    slot = i % 2
    load = functools.partial(_tile_dma, to_vmem=True, acc_ref=acc_ref, sems=in_sem, main_hbm=x_hbm,
                             meta_hbm=meta_hbm, layout=in_layout)
    store = functools.partial(_tile_dma, to_vmem=False, acc_ref=acc_ref, sems=out_sem, main_hbm=o_hbm,
                              meta_hbm=None, layout=out_layout)

    def group_rows(r):
        return pl.ds(pl.multiple_of(r * group, group), group)

    @pl.when((i == 0) & (j == 0))
    def _():
        load("start", t=0, slot=0)

    @pl.when(j == 0)
    def _():
        load("wait", t=i, slot=slot)

        def rows_in(r, carry):
            rows = group_rows(r)
            h_ref[rows, :] = _rms_norm(acc_ref[slot, rows, :], g_ref[...]).astype(BF16)
            return carry

        lax.fori_loop(0, tile // group, rows_in, 0, unroll=ROW_GROUP_UNROLL)

    @pl.when((j == 1) & (i + 1 < n_row_tiles))
    def _():
        @pl.when(i >= 1)
        def _():
            store("wait", t=i - 1, slot=1 - slot)
        load("start", t=i + 1, slot=1 - slot)

    h = h_ref[...]
    gate = _dot(h, wg_ref[...].astype(BF16))
    up = _dot(h, wu_ref[...].astype(BF16))
    act = (0.5 * gate * jax.nn.sigmoid(gate) * up).astype(BF16)
    acc_ref[slot] += _dot(act, wd_ref[...].astype(BF16))

    @pl.when(j == last_j)
    def _():
        def rows_out(r, carry):
            rows = group_rows(r)
            y = _rms_norm(acc_ref[slot, rows, :], g2_ref[...])
            if final_norm:
                acc_ref[slot, rows, :] = y
            if emit_h:
                hn_ref[rows, :] = y.astype(BF16)
            return carry

        if final_norm or emit_h:
            lax.fori_loop(0, tile // group, rows_out, 0, unroll=ROW_GROUP_UNROLL)
        store("start", t=i, slot=slot)

    @pl.when((i == n_row_tiles - 1) & (j == last_j))
    def _():
        if n_row_tiles > 1:
            store("wait", t=i - 1, slot=1 - slot)
        store("wait", t=i, slot=slot)


def _ffn_half(x, meta, g, w_gate, w_up, w_down, layer, half, *, rows, seq_len, assemble=False, strip=False,
              final_g=None, next_g=None):
    d = x.shape[-1]
    dff = w_gate.shape[-1]
    n_row_tiles = rows // FFN_ROW_TILE
    n_ff = dff // FF_TILE
    assert n_row_tiles * FFN_ROW_TILE == rows and n_ff * FF_TILE == dff and n_ff >= 2
    assert FFN_ROW_TILE % V7X_BF16_SUBLANES == 0 and not (final_g is not None and next_g is not None)
    layout = _row_pieces(FFN_ROW_TILE, seq_len, N_META)
    n_sems = max(len(p) for p in layout[0])
    g2 = final_g if final_g is not None else next_g
    g2 = jnp.ones((1, d), F32) if g2 is None else g2.reshape(1, d)
    batch = rows // seq_len
    out_shape = [jax.ShapeDtypeStruct((batch, seq_len - N_META, d) if strip else (rows, d), F32)]
    out_specs = [pl.BlockSpec(memory_space=pl.ANY)]
    if next_g is not None:
        out_shape.append(jax.ShapeDtypeStruct((rows, d), BF16))
        out_specs.append(pl.BlockSpec((FFN_ROW_TILE, d), lambda i, j: (i, 0)))
    out = pl.pallas_call(
        functools.partial(_ffn_kernel, n_row_tiles=n_row_tiles, in_layout=layout if assemble else None,
                          out_layout=layout if strip else None, final_norm=final_g is not None,
                          emit_h=next_g is not None),
        grid=(n_row_tiles, n_ff),
        in_specs=[
            pl.BlockSpec(memory_space=pl.ANY),
            pl.BlockSpec(memory_space=pl.ANY),
            pl.BlockSpec((1, d), lambda i, j: (0, 0)),
            pl.BlockSpec((None, None, d, FF_TILE), lambda i, j: (layer, half, 0, j)),
            pl.BlockSpec((None, None, d, FF_TILE), lambda i, j: (layer, half, 0, j)),
            pl.BlockSpec((None, None, FF_TILE, d), lambda i, j: (layer, half, j, 0)),
            pl.BlockSpec((1, d), lambda i, j: (0, 0)),
        ],
        out_specs=out_specs,
        out_shape=out_shape,
        scratch_shapes=[
            pltpu.VMEM((2, FFN_ROW_TILE, d), F32),
            pltpu.VMEM((FFN_ROW_TILE, d), BF16),
            pltpu.SemaphoreType.DMA((2, n_sems)),
            pltpu.SemaphoreType.DMA((2, n_sems)),
        ],
        compiler_params=_params("arbitrary", "arbitrary"),
        name="ffn_half",
    )(x, meta, g.reshape(1, d), w_gate, w_up, w_down, g2)
    return out if next_g is not None else out[0]


def _gla_in_kernel(h_ref, wt_ref, wa_ref, wb_ref, bb_ref, proj_ref, lg_ref, low_ref):
    @pl.when(pl.program_id(1) == 0)
    def _():
        low_ref[...] = _dot_nt(h_ref[...], wa_ref[...].astype(BF16)).astype(BF16)

    z = _dot(low_ref[...], wb_ref[...]) + bb_ref[...]
    log_sig = jnp.minimum(z, 0.0) - jnp.log(1.0 + jnp.exp(-jnp.abs(z)))
    lg_ref[...] = log_sig / GLA_GATE_NORM
    proj_ref[...] = _dot_nt(h_ref[...], wt_ref[0].astype(BF16)).astype(proj_ref.dtype)


def _gla_in(h, w_in_t, layer, w_a_t, w_b, b_b, qkv_cols, r_start, r_cols):
    m, d = h.shape
    dk = w_b.shape[1]
    n_qkv = qkv_cols // GLA_IN_TILE
    n_steps = n_qkv + r_cols // GLA_IN_TILE
    n_lg = dk // GLA_LG_TILE
    assert n_qkv * GLA_IN_TILE == qkv_cols and r_cols % GLA_IN_TILE == 0 and m % FFN_ROW_TILE == 0
    assert n_lg * GLA_LG_TILE == dk and n_lg <= n_steps and r_start % 8 == 0

    def w_rows(i, j):
        row = jnp.where(j < n_qkv, j * GLA_IN_TILE, r_start + (j - n_qkv) * GLA_IN_TILE)
        return (layer, pl.multiple_of(row, 8), 0)

    def lg_block(i, j):
        return jnp.minimum(j, n_lg - 1)

    return pl.pallas_call(
        _gla_in_kernel,
        grid=(m // FFN_ROW_TILE, n_steps),
        in_specs=[
            pl.BlockSpec((FFN_ROW_TILE, d), lambda i, j: (i, 0)),
            pl.BlockSpec((pl.Element(1), pl.Element(GLA_IN_TILE), pl.Element(d)), w_rows),
            pl.BlockSpec((V7X_LANES, d), lambda i, j: (0, 0)),
            pl.BlockSpec((V7X_LANES, GLA_LG_TILE), lambda i, j: (0, lg_block(i, j))),
            pl.BlockSpec((1, GLA_LG_TILE), lambda i, j: (0, lg_block(i, j))),
        ],
        out_specs=[
            pl.BlockSpec((FFN_ROW_TILE, GLA_IN_TILE), lambda i, j: (i, j)),
            pl.BlockSpec((FFN_ROW_TILE, GLA_LG_TILE), lambda i, j: (i, lg_block(i, j))),
        ],
        out_shape=[
            jax.ShapeDtypeStruct((m, qkv_cols + r_cols), BF16),
            jax.ShapeDtypeStruct((m, dk), F32),
        ],
        scratch_shapes=[pltpu.VMEM((FFN_ROW_TILE, V7X_LANES), BF16)],
        compiler_params=_params("parallel", "arbitrary"),
        name="gla_in",
    )(h, w_in_t, w_a_t, w_b, b_b.reshape(1, dk))


def _gla_level_masks(chunk):
    i = np.arange(chunk)[:, None]
    j = np.arange(chunk)[None, :]
    masks = []
    for lev in range(chunk.bit_length() - 1):
        same_block = (i >> (lev + 1)) == (j >> (lev + 1))
        masks.append(same_block & (((i >> lev) & 1) == 1) & (((j >> lev) & 1) == 0))
    masks.append(i == j)
    return np.stack(masks).astype(np.float32)


def _gla_core_kernel(q_ref, k_ref, v_ref, r_ref, lg_ref, tri_ref, lm_ref, hn_ref, o_ref, st_ref,
                     *, seq_len, q_scale):
    c = pl.program_id(2)
    chunk, dk = q_ref.shape[1], q_ref.shape[2]

    @pl.when(c == 0)
    def _():
        st_ref[...] = jnp.zeros_like(st_ref)

    row = lax.broadcasted_iota(jnp.int32, (chunk, 1), 0)
    valid = row < (seq_len - c * chunk)
    q32 = jnp.where(valid, q_ref[0].astype(F32), 0.0)
    k32 = jnp.where(valid, k_ref[0].astype(F32), 0.0)
    q = q32.astype(BF16)
    k = k32.astype(BF16)
    v = jnp.where(valid, v_ref[0].astype(F32), 0.0).astype(BF16)
    lg = jnp.where(valid, lg_ref[0], 0.0) * LOG2E

    lg_hi = lg.astype(BF16)
    lg_lo = (lg - lg_hi.astype(F32)).astype(BF16)
    tri = tri_ref[...]
    b = _dot(tri, lg_hi) + _dot(tri, lg_lo)

    n_levels = chunk.bit_length() - 1
    a = jnp.sum(q32 * k32, axis=-1, keepdims=True) * lm_ref[n_levels]
    for lev in range(n_levels):
        s = 1 << lev
        if s == 1:
            e = jnp.where((row & 1) == 1, lg, 0.0)
        elif s == 2:
            nxt = pltpu.roll(lg, chunk - 1, 0)
            prv = pltpu.roll(lg, 1, 0)
            pos = row & 3
            e = jnp.where(pos == 0, nxt, jnp.where(pos == 2, lg, jnp.where(pos == 3, lg + prv, 0.0)))
        else:
            b3 = b.reshape(chunk // (2 * s), 2 * s, dk)
            e = (-jnp.abs(b3 - b3[:, s - 1:s, :])).reshape(chunk, dk)
        f = jnp.exp2(e).astype(BF16)
        a = a + _dot_nt(q * f, k * f) * lm_ref[lev]

    b_last = b[chunk - 1:chunk, :]
    qs = q * jnp.exp2(b).astype(BF16)
    ks = k * jnp.exp2(b_last - b).astype(BF16)
    st = st_ref[...]
    o = (_dot(a.astype(BF16), v) + _dot_nt(qs, st.astype(BF16))) * q_scale
    st_ref[...] = st * jnp.exp2(b_last) + _dot_tn(v, ks)

    o = _rms_norm(o, hn_ref[...])
    r = r_ref[0].astype(F32)
    o_ref[0] = (o * (r * jax.nn.sigmoid(r))).astype(o_ref.dtype)


def _gla_core(proj, lg, head_norm, batch, seq_len):
    dk_all = lg.shape[-1]
    dk = dk_all // GLA_HEADS
    dv_all = (proj.shape[-1] - 2 * dk_all) // 2
    dv = dv_all // GLA_HEADS
    proj = proj.reshape(batch, seq_len, proj.shape[-1])
    lg = lg.reshape(batch, seq_len, dk_all)
    c = GLA_CHUNK
    tri = jnp.asarray(np.tril(np.ones((c, c), np.float32)), BF16)
    level_masks = jnp.asarray(_gla_level_masks(c))
    k_blk = dk_all // dk
    v_blk = 2 * dk_all // dv
    r_blk = (2 * dk_all + dv_all) // dv
    out = pl.pallas_call(
        functools.partial(_gla_core_kernel, seq_len=seq_len, q_scale=dk ** -0.5),
        grid=(batch, GLA_HEADS, pl.cdiv(seq_len, c)),
        in_specs=[
            pl.BlockSpec((1, c, dk), lambda b, h, i: (b, i, h)),
            pl.BlockSpec((1, c, dk), lambda b, h, i: (b, i, k_blk + h)),
            pl.BlockSpec((1, c, dv), lambda b, h, i: (b, i, v_blk + h)),
            pl.BlockSpec((1, c, dv), lambda b, h, i: (b, i, r_blk + h)),
            pl.BlockSpec((1, c, dk), lambda b, h, i: (b, i, h)),
            pl.BlockSpec((c, c), lambda b, h, i: (0, 0)),
            pl.BlockSpec(level_masks.shape, lambda b, h, i: (0, 0, 0)),
            pl.BlockSpec((1, dv), lambda b, h, i: (0, 0)),
        ],
        out_specs=pl.BlockSpec((1, c, dv), lambda b, h, i: (b, i, h)),
        out_shape=jax.ShapeDtypeStruct((batch, seq_len, dv_all), BF16),
        scratch_shapes=[pltpu.VMEM((dv, dk), F32)],
        compiler_params=_params("parallel", "parallel", "arbitrary"),
        name="gla_core",
    )(proj, proj, proj, proj, lg, tri, level_masks, head_norm.reshape(1, dv))
    return out.reshape(batch * seq_len, dv_all)


def _proj_add_kernel(x_ref, a_ref, w_ref, o_ref):
    o_ref[...] = x_ref[...] + _dot(a_ref[...], w_ref[...])


def _proj_add(x, a, w):
    m, d = x.shape
    kdim = a.shape[1]
    return pl.pallas_call(
        _proj_add_kernel,
        grid=(m // ROW_TILE,),
        in_specs=[
            pl.BlockSpec((ROW_TILE, d), lambda i: (i, 0)),
            pl.BlockSpec((ROW_TILE, kdim), lambda i: (i, 0)),
            pl.BlockSpec((kdim, d), lambda i: (0, 0)),
        ],
        out_specs=pl.BlockSpec((ROW_TILE, d), lambda i: (i, 0)),
        out_shape=jax.ShapeDtypeStruct((m, d), F32),
        compiler_params=_params("parallel"),
        name="gla_out",
    )(x, a, w)


def _pool_kernel(x_ref, halo_ref, g_ref, w_ref, b_ref, sc_ref, o_ref, *, tiles_per_seq):
    i = pl.program_id(0)
    tile, d = x_ref.shape
    gw = w_ref.shape[1]
    first = (i % tiles_per_seq) == 0
    x = x_ref[...]
    h = _rms_norm(x, g_ref[...])
    h_halo = jnp.where(first, 0.0, _rms_norm(halo_ref[...], g_ref[...]))
    ext = jnp.concatenate([h_halo, h], axis=0)
    t = (i % tiles_per_seq) * tile + lax.broadcasted_iota(jnp.int32, (tile, 1), 0)
    for grp, win in enumerate(POOL_WINDOWS):
        cols = slice(grp * gw, (grp + 1) * gw)
        acc = ext[:, cols]
        step = 1
        while step < win:
            acc = acc + pltpu.roll(acc, step, 0)
            step *= 2
        cnt = jnp.minimum(t + 1, win).astype(F32)
        pooled = acc[POOL_HALO:, :] / cnt - h[:, cols]
        y = _dot(pooled.astype(BF16), w_ref[grp]) + b_ref[grp]
        o_ref[:, cols] = x[:, cols] + y * sc_ref[:, cols]


def _pool_mixer(x, g, w, b, scale, seq_len):
    m, d = x.shape
    groups, gw, _ = w.shape
    tiles_per_seq = seq_len // ROW_TILE
    halo_blocks = ROW_TILE // POOL_HALO
    return pl.pallas_call(
        functools.partial(_pool_kernel, tiles_per_seq=tiles_per_seq),
        grid=(m // ROW_TILE,),
        in_specs=[
            pl.BlockSpec((ROW_TILE, d), lambda i: (i, 0)),
            pl.BlockSpec((POOL_HALO, d), lambda i: (jnp.maximum(i * halo_blocks - 1, 0), 0)),
            pl.BlockSpec((1, d), lambda i: (0, 0)),
            pl.BlockSpec((groups, gw, gw), lambda i: (0, 0, 0)),
            pl.BlockSpec((groups, 1, gw), lambda i: (0, 0, 0)),
            pl.BlockSpec((1, d), lambda i: (0, 0)),
        ],
        out_specs=pl.BlockSpec((ROW_TILE, d), lambda i: (i, 0)),
        out_shape=jax.ShapeDtypeStruct((m, d), F32),
        compiler_params=_params("parallel"),
        name="pool_mixer",
    )(x, x, g.reshape(1, d), w, b.reshape(groups, 1, gw), scale.reshape(1, d))


def kernel(x, meta, ffn_norm, ffn_w_gate, ffn_w_up, ffn_w_down, gla_norm, gla_w_in, gla_w_lr, gla_b_lr,
           gla_head_norm, gla_w_out, pool_norm, pool_w, pool_b, pool_scale, final_norm):
    batch, seq, d = x.shape
    seq_len = seq + N_META
    rows = batch * seq_len
    depth = ffn_norm.shape[0]
    dk_all = gla_w_lr.shape[-1]
    dv_all = gla_w_out.shape[1]
    assert rows % ROW_TILE == 0 and seq_len % ROW_TILE == 0
    meta = meta.astype(x.dtype)
    w_in_t = jnp.swapaxes(gla_w_in, 1, 2)
    ffn = functools.partial(_ffn_half, meta=meta, w_gate=ffn_w_gate, w_up=ffn_w_up, w_down=ffn_w_down,
                            rows=rows, seq_len=seq_len)

    xs = x
    for i in range(depth):
        j = i // 2
        gla_layer = i % 2 == 0
        first, last = i == 0, i == depth - 1
        out = ffn(xs, g=ffn_norm[i, 0], layer=i, half=0, assemble=first, next_g=gla_norm[j] if gla_layer else None)
        if gla_layer:
            xs, hn = out
            lr0 = 2 * dk_all + dv_all
            pad_rank = ((0, V7X_LANES - GLA_GATE_RANK), (0, 0))
            w_a_t = jnp.pad(w_in_t[j, lr0:lr0 + GLA_GATE_RANK], pad_rank)
            w_b = jnp.pad(gla_w_lr[j], pad_rank).astype(BF16)
            proj, lg = _gla_in(hn, w_in_t, j, w_a_t, w_b, gla_b_lr[j], lr0, lr0 + GLA_GATE_RANK, dv_all)
            gated = _gla_core(proj, lg, gla_head_norm[j], batch, seq_len)
            xs = _proj_add(xs, gated, gla_w_out[j].astype(BF16))
        else:
            xs = _pool_mixer(out, pool_norm[j], pool_w[j].astype(BF16), pool_b[j], pool_scale[j], seq_len)
        xs = ffn(xs, g=ffn_norm[i, 1], layer=i, half=1, strip=last, final_g=final_norm if last else None)
    return xs
```

```python
import functools
import math

import jax
import jax.numpy as jnp
import numpy as np
from jax import lax
from jax.experimental import pallas as pl
from jax.experimental.pallas import tpu as pltpu

F32 = jnp.float32
BF16 = jnp.bfloat16

EPS = 1e-6
N_META = 16
GLA_HEADS = 4
GLA_GATE_RANK = 16
GLA_GATE_NORM = 16.0
POOL_WINDOWS = (2, 4, 8, 16)
LOG2E = math.log2(math.e)

V7X_LANES = 128
V7X_BF16_SUBLANES = 16
V7X_VMEM_LIMIT_BYTES = 60 * 1024 * 1024

ROW_TILE = 688
FFN_ROW_TILE = 1376
FF_TILES = (512, 256)
FF_SUB = 256
GLA_CHUNK = 256
GLA_IN_TILE = 1024
GLA_LG_TILE = 256
POOL_HALO = 16
ROW_GROUP_UNROLL = 8


def _params(*semantics):
    return pltpu.CompilerParams(dimension_semantics=semantics,
                                vmem_limit_bytes=V7X_VMEM_LIMIT_BYTES)


def _rms_norm(x, g):
    return x * lax.rsqrt(jnp.mean(x * x, axis=-1, keepdims=True) + EPS) * g


def _dot(a, b):
    return jnp.dot(a, b, preferred_element_type=F32)


def _dot_nt(a, b):
    return lax.dot_general(a, b, (((1,), (1,)), ((), ())), preferred_element_type=F32)


def _dot_tn(a, b):
    return lax.dot_general(a, b, (((0,), (0,)), ((), ())), preferred_element_type=F32)


def _row_pieces(tile, seq_len, n_meta):
    period = math.lcm(tile, seq_len)
    phases = []
    for p in range(period // tile):
        pieces, r, end = [], p * tile, (p + 1) * tile
        while r < end:
            b, t = divmod(r, seq_len)
            if t < n_meta:
                n = min(n_meta - t, end - r)
                pieces.append(("meta", b, t, r - p * tile, n))
            else:
                n = min(seq_len - t, end - r)
                pieces.append(("seq", b, t - n_meta, r - p * tile, n))
            r += n
        phases.append(pieces)
    return phases, period // seq_len


def _tile_dma(action, to_vmem, t, slot, acc_ref, sems, main_hbm, meta_hbm, layout):
    tile = acc_ref.shape[1]

    def run(hbm, vmem, sem):
        copy = pltpu.make_async_copy(hbm, vmem, sem) if to_vmem else pltpu.make_async_copy(vmem, hbm, sem)
        if action == "start":
            copy.start()
        else:
            copy.wait()

    if layout is None:
        run(main_hbm.at[pl.ds(t * tile, tile)], acc_ref.at[slot], sems.at[slot, 0])
        return
    phases, batches = layout
    for p, pieces in enumerate(phases):
        @pl.when(t % len(phases) == p)
        def _():
            base = (t // len(phases)) * batches
            for idx, (kind, b, src, dst, n) in enumerate(pieces):
                vmem = acc_ref.at[slot, pl.ds(dst, n)]
                if kind == "seq":
                    run(main_hbm.at[base + b, pl.ds(src, n)], vmem, sems.at[slot, idx])
                elif to_vmem:
                    run(meta_hbm.at[pl.ds(src, n)], vmem, sems.at[slot, idx])


def _ffn_kernel(*refs, n_row_tiles, in_layout, out_layout, final_norm, emit_h):
    if emit_h:
        (x_hbm, meta_hbm, g_ref, wg_ref, wu_ref, wd_ref, g2_ref, o_hbm, hn_ref,
         acc_ref, h_ref, in_sem, out_sem) = refs
    else:
        (x_hbm, meta_hbm, g_ref, wg_ref, wu_ref, wd_ref, g2_ref, o_hbm,
         acc_ref, h_ref, in_sem, out_sem) = refs
    i = pl.program_id(0)
    j = pl.program_id(1)
    last_j = pl.num_programs(1) - 1
    tile = acc_ref.shape[1]
    group = V7X_BF16_SUBLANES
    slot = i % 2
    load = functools.partial(_tile_dma, to_vmem=True, acc_ref=acc_ref, sems=in_sem, main_hbm=x_hbm,
                             meta_hbm=meta_hbm, layout=in_layout)
    store = functools.partial(_tile_dma, to_vmem=False, acc_ref=acc_ref, sems=out_sem, main_hbm=o_hbm,
                              meta_hbm=None, layout=out_layout)

    def group_rows(r):
        return pl.ds(pl.multiple_of(r * group, group), group)

    @pl.when((i == 0) & (j == 0))
    def _():
        load("start", t=0, slot=0)

    @pl.when(j == 0)
    def _():
        load("wait", t=i, slot=slot)

        def rows_in(r, carry):
            rows = group_rows(r)
            h_ref[rows, :] = _rms_norm(acc_ref[slot, rows, :], g_ref[...]).astype(BF16)
            return carry

        lax.fori_loop(0, tile // group, rows_in, 0, unroll=ROW_GROUP_UNROLL)

    @pl.when((j == 1) & (i + 1 < n_row_tiles))
    def _():
        @pl.when(i >= 1)
        def _():
            store("wait", t=i - 1, slot=1 - slot)
        load("start", t=i + 1, slot=1 - slot)

    for c in range(wd_ref.shape[0] // FF_SUB):
        cols = slice(c * FF_SUB, (c + 1) * FF_SUB)
        h = h_ref[...]
        gate = _dot(h, wg_ref[:, cols].astype(BF16))
        up = _dot(h, wu_ref[:, cols].astype(BF16))
        act = (0.5 * gate * jax.nn.sigmoid(gate) * up).astype(BF16)
        acc_ref[slot] += _dot(act, wd_ref[cols, :].astype(BF16))

    @pl.when(j == last_j)
    def _():
        def rows_out(r, carry):
            rows = group_rows(r)
            hn_ref[rows, :] = _rms_norm(acc_ref[slot, rows, :], g2_ref[...]).astype(BF16)
            return carry

        if emit_h:
            lax.fori_loop(0, tile // group, rows_out, 0, unroll=ROW_GROUP_UNROLL)
        if final_norm:
            for r in range(tile // group):
                rows = slice(r * group, (r + 1) * group)
                acc_ref[slot, rows, :] = _rms_norm(acc_ref[slot, rows, :], g2_ref[...])
        store("start", t=i, slot=slot)

    @pl.when((i == n_row_tiles - 1) & (j == last_j))
    def _():
        if n_row_tiles > 1:
            store("wait", t=i - 1, slot=1 - slot)
        store("wait", t=i, slot=slot)


def _ffn_ff_tile(d, emit_h):
    fixed = 2 * FFN_ROW_TILE * d * 4 + FFN_ROW_TILE * d * 2 + (2 * FFN_ROW_TILE * d * 2 if emit_h else 0)
    temps = 2 * FFN_ROW_TILE * FF_SUB * 4 + FFN_ROW_TILE * FF_SUB * 2 + 3 * d * FF_SUB * 2
    for ff_tile in FF_TILES:
        if fixed + temps + 3 * 2 * d * ff_tile * 4 <= V7X_VMEM_LIMIT_BYTES:
            return ff_tile
    raise ValueError("no D_FF chunk fits VMEM")


def _ffn_half(x, meta, g, w_gate, w_up, w_down, layer, half, *, rows, seq_len, assemble=False, strip=False,
              final_g=None, next_g=None):
    d = x.shape[-1]
    dff = w_gate.shape[-1]
    n_row_tiles = rows // FFN_ROW_TILE
    ff_tile = _ffn_ff_tile(d, emit_h=next_g is not None)
    n_ff = dff // ff_tile
    assert n_row_tiles * FFN_ROW_TILE == rows and n_ff * ff_tile == dff and n_ff >= 2
    assert FFN_ROW_TILE % V7X_BF16_SUBLANES == 0 and not (final_g is not None and next_g is not None)
    layout = _row_pieces(FFN_ROW_TILE, seq_len, N_META)
    n_sems = max(len(p) for p in layout[0])
    g2 = final_g if final_g is not None else next_g
    g2 = jnp.ones((1, d), F32) if g2 is None else g2.reshape(1, d)
    batch = rows // seq_len
    out_shape = [jax.ShapeDtypeStruct((batch, seq_len - N_META, d) if strip else (rows, d), F32)]
    out_specs = [pl.BlockSpec(memory_space=pl.ANY)]
    if next_g is not None:
        out_shape.append(jax.ShapeDtypeStruct((rows, d), BF16))
        out_specs.append(pl.BlockSpec((FFN_ROW_TILE, d), lambda i, j: (i, 0)))
    out = pl.pallas_call(
        functools.partial(_ffn_kernel, n_row_tiles=n_row_tiles, in_layout=layout if assemble else None,
                          out_layout=layout if strip else None, final_norm=final_g is not None,
                          emit_h=next_g is not None),
        grid=(n_row_tiles, n_ff),
        in_specs=[
            pl.BlockSpec(memory_space=pl.ANY),
            pl.BlockSpec(memory_space=pl.ANY),
            pl.BlockSpec((1, d), lambda i, j: (0, 0)),
            pl.BlockSpec((None, None, d, ff_tile), lambda i, j: (layer, half, 0, j)),
            pl.BlockSpec((None, None, d, ff_tile), lambda i, j: (layer, half, 0, j)),
            pl.BlockSpec((None, None, ff_tile, d), lambda i, j: (layer, half, j, 0)),
            pl.BlockSpec((1, d), lambda i, j: (0, 0)),
        ],
        out_specs=out_specs,
        out_shape=out_shape,
        scratch_shapes=[
            pltpu.VMEM((2, FFN_ROW_TILE, d), F32),
            pltpu.VMEM((FFN_ROW_TILE, d), BF16),
            pltpu.SemaphoreType.DMA((2, n_sems)),
            pltpu.SemaphoreType.DMA((2, n_sems)),
        ],
        compiler_params=_params("arbitrary", "arbitrary"),
        name="ffn_half",
    )(x, meta, g.reshape(1, d), w_gate, w_up, w_down, g2)
    return out if next_g is not None else out[0]


def _gla_in_kernel(h_ref, wt_ref, wa_ref, wb_ref, bb_ref, proj_ref, lg_ref, low_ref):
    @pl.when(pl.program_id(1) == 0)
    def _():
        low_ref[...] = _dot_nt(h_ref[...], wa_ref[...].astype(BF16)).astype(BF16)

    z = _dot(low_ref[...], wb_ref[...]) + bb_ref[...]
    log_sig = jnp.minimum(z, 0.0) - jnp.log(1.0 + jnp.exp(-jnp.abs(z)))
    lg_ref[...] = log_sig / GLA_GATE_NORM
    proj_ref[...] = _dot_nt(h_ref[...], wt_ref[0].astype(BF16)).astype(proj_ref.dtype)


def _gla_in(h, w_in_t, layer, w_a_t, w_b, b_b, qkv_cols, r_start, r_cols):
    m, d = h.shape
    dk = w_b.shape[1]
    n_qkv = qkv_cols // GLA_IN_TILE
    n_steps = n_qkv + r_cols // GLA_IN_TILE
    n_lg = dk // GLA_LG_TILE
    assert n_qkv * GLA_IN_TILE == qkv_cols and r_cols % GLA_IN_TILE == 0 and m % FFN_ROW_TILE == 0
    assert n_lg * GLA_LG_TILE == dk and n_lg <= n_steps and r_start % 8 == 0

    def w_rows(i, j):
        row = jnp.where(j < n_qkv, j * GLA_IN_TILE, r_start + (j - n_qkv) * GLA_IN_TILE)
        return (layer, pl.multiple_of(row, 8), 0)

    def lg_block(i, j):
        return jnp.minimum(j, n_lg - 1)

    return pl.pallas_call(
        _gla_in_kernel,
        grid=(m // FFN_ROW_TILE, n_steps),
        in_specs=[
            pl.BlockSpec((FFN_ROW_TILE, d), lambda i, j: (i, 0)),
            pl.BlockSpec((pl.Element(1), pl.Element(GLA_IN_TILE), pl.Element(d)), w_rows),
            pl.BlockSpec((V7X_LANES, d), lambda i, j: (0, 0)),
            pl.BlockSpec((V7X_LANES, GLA_LG_TILE), lambda i, j: (0, lg_block(i, j))),
            pl.BlockSpec((1, GLA_LG_TILE), lambda i, j: (0, lg_block(i, j))),
        ],
        out_specs=[
            pl.BlockSpec((FFN_ROW_TILE, GLA_IN_TILE), lambda i, j: (i, j)),
            pl.BlockSpec((FFN_ROW_TILE, GLA_LG_TILE), lambda i, j: (i, lg_block(i, j))),
        ],
        out_shape=[
            jax.ShapeDtypeStruct((m, qkv_cols + r_cols), BF16),
            jax.ShapeDtypeStruct((m, dk), F32),
        ],
        scratch_shapes=[pltpu.VMEM((FFN_ROW_TILE, V7X_LANES), BF16)],
        compiler_params=_params("parallel", "arbitrary"),
        name="gla_in",
    )(h, w_in_t, w_a_t, w_b, b_b.reshape(1, dk))


def _gla_level_masks(chunk):
    i = np.arange(chunk)[:, None]
    j = np.arange(chunk)[None, :]
    masks = []
    for lev in range(chunk.bit_length() - 1):
        same_block = (i >> (lev + 1)) == (j >> (lev + 1))
        masks.append(same_block & (((i >> lev) & 1) == 1) & (((j >> lev) & 1) == 0))
    masks.append(i == j)
    return np.stack(masks).astype(np.float32)


def _gla_core_kernel(q_ref, k_ref, v_ref, r_ref, lg_ref, tri_ref, lm_ref, hn_ref, o_ref, st_ref,
                     *, seq_len, q_scale):
    c = pl.program_id(2)
    chunk, dk = q_ref.shape[1], q_ref.shape[2]

    @pl.when(c == 0)
    def _():
        st_ref[...] = jnp.zeros_like(st_ref)

    row = lax.broadcasted_iota(jnp.int32, (chunk, 1), 0)
    valid = row < (seq_len - c * chunk)
    q32 = jnp.where(valid, q_ref[0].astype(F32), 0.0)
    k32 = jnp.where(valid, k_ref[0].astype(F32), 0.0)
    q = q32.astype(BF16)
    k = k32.astype(BF16)
    v = jnp.where(valid, v_ref[0].astype(F32), 0.0).astype(BF16)
    lg = jnp.where(valid, lg_ref[0], 0.0) * LOG2E

    lg_hi = lg.astype(BF16)
    lg_lo = (lg - lg_hi.astype(F32)).astype(BF16)
    tri = tri_ref[...]
    b = _dot(tri, lg_hi) + _dot(tri, lg_lo)

    n_levels = chunk.bit_length() - 1
    a = jnp.sum(q32 * k32, axis=-1, keepdims=True) * lm_ref[n_levels]
    for lev in range(n_levels):
        s = 1 << lev
        if s == 1:
            e = jnp.where((row & 1) == 1, lg, 0.0)
        elif s == 2:
            nxt = pltpu.roll(lg, chunk - 1, 0)
            prv = pltpu.roll(lg, 1, 0)
            pos = row & 3
            e = jnp.where(pos == 0, nxt, jnp.where(pos == 2, lg, jnp.where(pos == 3, lg + prv, 0.0)))
        else:
            b3 = b.reshape(chunk // (2 * s), 2 * s, dk)
            e = (-jnp.abs(b3 - b3[:, s - 1:s, :])).reshape(chunk, dk)
        f = jnp.exp2(e).astype(BF16)
        a = a + _dot_nt(q * f, k * f) * lm_ref[lev]

    b_last = b[chunk - 1:chunk, :]
    qs = q * jnp.exp2(b).astype(BF16)
    ks = k * jnp.exp2(b_last - b).astype(BF16)
    st = st_ref[...]
    o = (_dot(a.astype(BF16), v) + _dot_nt(qs, st.astype(BF16))) * q_scale
    st_ref[...] = st * jnp.exp2(b_last) + _dot_tn(v, ks)

    o = _rms_norm(o, hn_ref[...])
    r = r_ref[0].astype(F32)
    o_ref[0] = (o * (r * jax.nn.sigmoid(r))).astype(o_ref.dtype)


def _gla_core(proj, lg, head_norm, batch, seq_len):
    dk_all = lg.shape[-1]
    dk = dk_all // GLA_HEADS
    dv_all = (proj.shape[-1] - 2 * dk_all) // 2
    dv = dv_all // GLA_HEADS
    proj = proj.reshape(batch, seq_len, proj.shape[-1])
    lg = lg.reshape(batch, seq_len, dk_all)
    c = GLA_CHUNK
    tri = jnp.asarray(np.tril(np.ones((c, c), np.float32)), BF16)
    level_masks = jnp.asarray(_gla_level_masks(c))
    k_blk = dk_all // dk
    v_blk = 2 * dk_all // dv
    r_blk = (2 * dk_all + dv_all) // dv
    out = pl.pallas_call(
        functools.partial(_gla_core_kernel, seq_len=seq_len, q_scale=dk ** -0.5),
        grid=(batch, GLA_HEADS, pl.cdiv(seq_len, c)),
        in_specs=[
            pl.BlockSpec((1, c, dk), lambda b, h, i: (b, i, h)),
            pl.BlockSpec((1, c, dk), lambda b, h, i: (b, i, k_blk + h)),
            pl.BlockSpec((1, c, dv), lambda b, h, i: (b, i, v_blk + h)),
            pl.BlockSpec((1, c, dv), lambda b, h, i: (b, i, r_blk + h)),
            pl.BlockSpec((1, c, dk), lambda b, h, i: (b, i, h)),
            pl.BlockSpec((c, c), lambda b, h, i: (0, 0)),
            pl.BlockSpec(level_masks.shape, lambda b, h, i: (0, 0, 0)),
            pl.BlockSpec((1, dv), lambda b, h, i: (0, 0)),
        ],
        out_specs=pl.BlockSpec((1, c, dv), lambda b, h, i: (b, i, h)),
        out_shape=jax.ShapeDtypeStruct((batch, seq_len, dv_all), BF16),
        scratch_shapes=[pltpu.VMEM((dv, dk), F32)],
        compiler_params=_params("parallel", "parallel", "arbitrary"),
        name="gla_core",
    )(proj, proj, proj, proj, lg, tri, level_masks, head_norm.reshape(1, dv))
    return out.reshape(batch * seq_len, dv_all)


def _proj_add_kernel(x_ref, a_ref, w_ref, o_ref):
    o_ref[...] = x_ref[...] + _dot(a_ref[...], w_ref[...])


def _proj_add(x, a, w):
    m, d = x.shape
    kdim = a.shape[1]
    return pl.pallas_call(
        _proj_add_kernel,
        grid=(m // ROW_TILE,),
        in_specs=[
            pl.BlockSpec((ROW_TILE, d), lambda i: (i, 0)),
            pl.BlockSpec((ROW_TILE, kdim), lambda i: (i, 0)),
            pl.BlockSpec((kdim, d), lambda i: (0, 0)),
        ],
        out_specs=pl.BlockSpec((ROW_TILE, d), lambda i: (i, 0)),
        out_shape=jax.ShapeDtypeStruct((m, d), F32),
        compiler_params=_params("parallel"),
        name="gla_out",
    )(x, a, w)


def _pool_kernel(x_ref, halo_ref, g_ref, w_ref, b_ref, sc_ref, o_ref, *, tiles_per_seq):
    i = pl.program_id(0)
    tile, d = x_ref.shape
    gw = w_ref.shape[1]
    first = (i % tiles_per_seq) == 0
    x = x_ref[...]
    h = _rms_norm(x, g_ref[...])
    h_halo = jnp.where(first, 0.0, _rms_norm(halo_ref[...], g_ref[...]))
    ext = jnp.concatenate([h_halo, h], axis=0)
    t = (i % tiles_per_seq) * tile + lax.broadcasted_iota(jnp.int32, (tile, 1), 0)
    for grp, win in enumerate(POOL_WINDOWS):
        cols = slice(grp * gw, (grp + 1) * gw)
        acc = ext[:, cols]
        step = 1
        while step < win:
            acc = acc + pltpu.roll(acc, step, 0)
            step *= 2
        cnt = jnp.minimum(t + 1, win).astype(F32)
        pooled = acc[POOL_HALO:, :] / cnt - h[:, cols]
        y = _dot(pooled.astype(BF16), w_ref[grp]) + b_ref[grp]
        o_ref[:, cols] = x[:, cols] + y * sc_ref[:, cols]


def _pool_mixer(x, g, w, b, scale, seq_len):
    m, d = x.shape
    groups, gw, _ = w.shape
    tiles_per_seq = seq_len // ROW_TILE
    halo_blocks = ROW_TILE // POOL_HALO
    return pl.pallas_call(
        functools.partial(_pool_kernel, tiles_per_seq=tiles_per_seq),
        grid=(m // ROW_TILE,),
        in_specs=[
            pl.BlockSpec((ROW_TILE, d), lambda i: (i, 0)),
            pl.BlockSpec((POOL_HALO, d), lambda i: (jnp.maximum(i * halo_blocks - 1, 0), 0)),
            pl.BlockSpec((1, d), lambda i: (0, 0)),
            pl.BlockSpec((groups, gw, gw), lambda i: (0, 0, 0)),
            pl.BlockSpec((groups, 1, gw), lambda i: (0, 0, 0)),
            pl.BlockSpec((1, d), lambda i: (0, 0)),
        ],
        out_specs=pl.BlockSpec((ROW_TILE, d), lambda i: (i, 0)),
        out_shape=jax.ShapeDtypeStruct((m, d), F32),
        compiler_params=_params("parallel"),
        name="pool_mixer",
    )(x, x, g.reshape(1, d), w, b.reshape(groups, 1, gw), scale.reshape(1, d))


def kernel(x, meta, ffn_norm, ffn_w_gate, ffn_w_up, ffn_w_down, gla_norm, gla_w_in, gla_w_lr, gla_b_lr,
           gla_head_norm, gla_w_out, pool_norm, pool_w, pool_b, pool_scale, final_norm):
    batch, seq, d = x.shape
    seq_len = seq + N_META
    rows = batch * seq_len
    depth = ffn_norm.shape[0]
    dk_all = gla_w_lr.shape[-1]
    dv_all = gla_w_out.shape[1]
    assert rows % ROW_TILE == 0 and seq_len % ROW_TILE == 0
    meta = meta.astype(x.dtype)
    w_in_t = jnp.swapaxes(gla_w_in, 1, 2)
    ffn = functools.partial(_ffn_half, meta=meta, w_gate=ffn_w_gate, w_up=ffn_w_up, w_down=ffn_w_down,
                            rows=rows, seq_len=seq_len)

    xs = x
    for i in range(depth):
        j = i // 2
        gla_layer = i % 2 == 0
        first, last = i == 0, i == depth - 1
        out = ffn(xs, g=ffn_norm[i, 0], layer=i, half=0, assemble=first, next_g=gla_norm[j] if gla_layer else None)
        if gla_layer:
            xs, hn = out
            lr0 = 2 * dk_all + dv_all
            pad_rank = ((0, V7X_LANES - GLA_GATE_RANK), (0, 0))
            w_a_t = jnp.pad(w_in_t[j, lr0:lr0 + GLA_GATE_RANK], pad_rank)
            w_b = jnp.pad(gla_w_lr[j], pad_rank).astype(BF16)
            proj, lg = _gla_in(hn, w_in_t, j, w_a_t, w_b, gla_b_lr[j], lr0, lr0 + GLA_GATE_RANK, dv_all)
            gated = _gla_core(proj, lg, gla_head_norm[j], batch, seq_len)
            xs = _proj_add(xs, gated, gla_w_out[j].astype(BF16))
        else:
            xs = _pool_mixer(out, pool_norm[j], pool_w[j].astype(BF16), pool_b[j], pool_scale[j], seq_len)
        xs = ffn(xs, g=ffn_norm[i, 1], layer=i, half=1, strip=last, final_g=final_norm if last else None)
    return xs
```

```python
import functools
import math

import jax
import jax.numpy as jnp
import numpy as np
from jax import lax
from jax.experimental import pallas as pl
from jax.experimental.pallas import tpu as pltpu

F32 = jnp.float32
BF16 = jnp.bfloat16

EPS = 1e-6
N_META = 16
GLA_HEADS = 4
GLA_GATE_RANK = 16
GLA_GATE_NORM = 16.0
POOL_WINDOWS = (2, 4, 8, 16)
LOG2E = math.log2(math.e)

V7X_LANES = 128
V7X_BF16_SUBLANES = 16
V7X_VMEM_LIMIT_BYTES = 60 * 1024 * 1024

ROW_TILE = 688
FFN_ROW_TILE = 1376
FF_TILES = (512, 256)
FF_SUB = 256
GLA_CHUNK = 256
GLA_IN_TILE = 1024
GLA_LG_TILE = 256
POOL_HALO = 16
ROW_GROUP_UNROLL = 8


def _params(*semantics):
    return pltpu.CompilerParams(dimension_semantics=semantics,
                                vmem_limit_bytes=V7X_VMEM_LIMIT_BYTES)


def _rms_norm(x, g):
    return x * lax.rsqrt(jnp.mean(x * x, axis=-1, keepdims=True) + EPS) * g


def _dot(a, b):
    return jnp.dot(a, b, preferred_element_type=F32)


def _dot_nt(a, b):
    return lax.dot_general(a, b, (((1,), (1,)), ((), ())), preferred_element_type=F32)


def _dot_tn(a, b):
    return lax.dot_general(a, b, (((0,), (0,)), ((), ())), preferred_element_type=F32)


def _row_pieces(tile, seq_len, n_meta):
    period = math.lcm(tile, seq_len)
    phases = []
    for p in range(period // tile):
        pieces, r, end = [], p * tile, (p + 1) * tile
        while r < end:
            b, t = divmod(r, seq_len)
            if t < n_meta:
                n = min(n_meta - t, end - r)
                pieces.append(("meta", b, t, r - p * tile, n))
            else:
                n = min(seq_len - t, end - r)
                pieces.append(("seq", b, t - n_meta, r - p * tile, n))
            r += n
        phases.append(pieces)
    return phases, period // seq_len


def _tile_dma(action, to_vmem, t, slot, acc_ref, sems, main_hbm, meta_hbm, layout):
    tile = acc_ref.shape[1]

    def run(hbm, vmem, sem):
        copy = pltpu.make_async_copy(hbm, vmem, sem) if to_vmem else pltpu.make_async_copy(vmem, hbm, sem)
        if action == "start":
            copy.start()
        else:
            copy.wait()

    if layout is None:
        run(main_hbm.at[pl.ds(t * tile, tile)], acc_ref.at[slot], sems.at[slot, 0])
        return
    phases, batches = layout
    for p, pieces in enumerate(phases):
        @pl.when(t % len(phases) == p)
        def _():
            base = (t // len(phases)) * batches
            for idx, (kind, b, src, dst, n) in enumerate(pieces):
                vmem = acc_ref.at[slot, pl.ds(dst, n)]
                if kind == "seq":
                    run(main_hbm.at[base + b, pl.ds(src, n)], vmem, sems.at[slot, idx])
                elif to_vmem:
                    run(meta_hbm.at[pl.ds(src, n)], vmem, sems.at[slot, idx])


def _ffn_kernel(*refs, n_row_tiles, in_layout, out_layout, final_norm, emit_h):
    if emit_h:
        (x_hbm, meta_hbm, g_ref, wg_ref, wu_ref, wd_ref, g2_ref, o_hbm, hn_hbm,
         acc_ref, h_ref, in_sem, out_sem, hn_sem) = refs
    else:
        (x_hbm, meta_hbm, g_ref, wg_ref, wu_ref, wd_ref, g2_ref, o_hbm,
         acc_ref, h_ref, in_sem, out_sem) = refs
    i = pl.program_id(0)
    j = pl.program_id(1)
    last_j = pl.num_programs(1) - 1
    tile = acc_ref.shape[1]
    group = V7X_BF16_SUBLANES
    slot = i % 2
    load = functools.partial(_tile_dma, to_vmem=True, acc_ref=acc_ref, sems=in_sem, main_hbm=x_hbm,
                             meta_hbm=meta_hbm, layout=in_layout)
    store = functools.partial(_tile_dma, to_vmem=False, acc_ref=acc_ref, sems=out_sem, main_hbm=o_hbm,
                              meta_hbm=None, layout=out_layout)

    def group_rows(r):
        return pl.ds(pl.multiple_of(r * group, group), group)

    def hn_copy(t):
        return pltpu.make_async_copy(h_ref, hn_hbm.at[pl.ds(t * tile, tile)], hn_sem.at[0])

    @pl.when((i == 0) & (j == 0))
    def _():
        load("start", t=0, slot=0)

    @pl.when(j == 0)
    def _():
        load("wait", t=i, slot=slot)
        if emit_h:
            @pl.when(i >= 1)
            def _():
                hn_copy(i - 1).wait()

        def rows_in(r, carry):
            rows = group_rows(r)
            h_ref[rows, :] = _rms_norm(acc_ref[slot, rows, :], g_ref[...]).astype(BF16)
            return carry

        lax.fori_loop(0, tile // group, rows_in, 0, unroll=ROW_GROUP_UNROLL)

    @pl.when((j == 1) & (i + 1 < n_row_tiles))
    def _():
        @pl.when(i >= 1)
        def _():
            store("wait", t=i - 1, slot=1 - slot)
        load("start", t=i + 1, slot=1 - slot)

    for c in range(wd_ref.shape[0] // FF_SUB):
        cols = slice(c * FF_SUB, (c + 1) * FF_SUB)
        h = h_ref[...]
        gate = _dot(h, wg_ref[:, cols].astype(BF16))
        up = _dot(h, wu_ref[:, cols].astype(BF16))
        act = (0.5 * gate * jax.nn.sigmoid(gate) * up).astype(BF16)
        acc_ref[slot] += _dot(act, wd_ref[cols, :].astype(BF16))

    @pl.when(j == last_j)
    def _():
        def rows_out(r, carry):
            rows = group_rows(r)
            h_ref[rows, :] = _rms_norm(acc_ref[slot, rows, :], g2_ref[...]).astype(BF16)
            return carry

        if emit_h:
            lax.fori_loop(0, tile // group, rows_out, 0, unroll=ROW_GROUP_UNROLL)
            hn_copy(i).start()
        if final_norm:
            for r in range(tile // group):
                rows = slice(r * group, (r + 1) * group)
                acc_ref[slot, rows, :] = _rms_norm(acc_ref[slot, rows, :], g2_ref[...])
        store("start", t=i, slot=slot)

    @pl.when((i == n_row_tiles - 1) & (j == last_j))
    def _():
        if n_row_tiles > 1:
            store("wait", t=i - 1, slot=1 - slot)
        store("wait", t=i, slot=slot)
        if emit_h:
            hn_copy(i).wait()


def _ffn_ff_tile(d):
    fixed = 2 * FFN_ROW_TILE * d * 4 + FFN_ROW_TILE * d * 2
    temps = 2 * FFN_ROW_TILE * FF_SUB * 4 + FFN_ROW_TILE * FF_SUB * 2 + 3 * d * FF_SUB * 2
    for ff_tile in FF_TILES:
        if fixed + temps + 3 * 2 * d * ff_tile * 4 <= V7X_VMEM_LIMIT_BYTES:
            return ff_tile
    raise ValueError("no D_FF chunk fits VMEM")


def _ffn_half(x, meta, g, w_gate, w_up, w_down, layer, half, *, rows, seq_len, assemble=False, strip=False,
              final_g=None, next_g=None):
    d = x.shape[-1]
    dff = w_gate.shape[-1]
    n_row_tiles = rows // FFN_ROW_TILE
    ff_tile = _ffn_ff_tile(d)
    n_ff = dff // ff_tile
    assert n_row_tiles * FFN_ROW_TILE == rows and n_ff * ff_tile == dff and n_ff >= 2
    assert FFN_ROW_TILE % V7X_BF16_SUBLANES == 0 and not (final_g is not None and next_g is not None)
    layout = _row_pieces(FFN_ROW_TILE, seq_len, N_META)
    n_sems = max(len(p) for p in layout[0])
    g2 = final_g if final_g is not None else next_g
    g2 = jnp.ones((1, d), F32) if g2 is None else g2.reshape(1, d)
    batch = rows // seq_len
    out_shape = [jax.ShapeDtypeStruct((batch, seq_len - N_META, d) if strip else (rows, d), F32)]
    out_specs = [pl.BlockSpec(memory_space=pl.ANY)]
    if next_g is not None:
        out_shape.append(jax.ShapeDtypeStruct((rows, d), BF16))
        out_specs.append(pl.BlockSpec(memory_space=pl.ANY))
    out = pl.pallas_call(
        functools.partial(_ffn_kernel, n_row_tiles=n_row_tiles, in_layout=layout if assemble else None,
                          out_layout=layout if strip else None, final_norm=final_g is not None,
                          emit_h=next_g is not None),
        grid=(n_row_tiles, n_ff),
        in_specs=[
            pl.BlockSpec(memory_space=pl.ANY),
            pl.BlockSpec(memory_space=pl.ANY),
            pl.BlockSpec((1, d), lambda i, j: (0, 0)),
            pl.BlockSpec((None, None, d, ff_tile), lambda i, j: (layer, half, 0, j)),
            pl.BlockSpec((None, None, d, ff_tile), lambda i, j: (layer, half, 0, j)),
            pl.BlockSpec((None, None, ff_tile, d), lambda i, j: (layer, half, j, 0)),
            pl.BlockSpec((1, d), lambda i, j: (0, 0)),
        ],
        out_specs=out_specs,
        out_shape=out_shape,
        scratch_shapes=[
            pltpu.VMEM((2, FFN_ROW_TILE, d), F32),
            pltpu.VMEM((FFN_ROW_TILE, d), BF16),
            pltpu.SemaphoreType.DMA((2, n_sems)),
            pltpu.SemaphoreType.DMA((2, n_sems)),
        ] + ([pltpu.SemaphoreType.DMA((1,))] if next_g is not None else []),
        compiler_params=_params("arbitrary", "arbitrary"),
        name="ffn_half",
    )(x, meta, g.reshape(1, d), w_gate, w_up, w_down, g2)
    return out if next_g is not None else out[0]


def _gla_in_kernel(h_ref, wt_ref, wa_ref, wb_ref, bb_ref, proj_ref, lg_ref, low_ref):
    @pl.when(pl.program_id(1) == 0)
    def _():
        low_ref[...] = _dot_nt(h_ref[...], wa_ref[...].astype(BF16)).astype(BF16)

    z = _dot(low_ref[...], wb_ref[...]) + bb_ref[...]
    log_sig = jnp.minimum(z, 0.0) - jnp.log(1.0 + jnp.exp(-jnp.abs(z)))
    lg_ref[...] = log_sig / GLA_GATE_NORM
    proj_ref[...] = _dot_nt(h_ref[...], wt_ref[0].astype(BF16)).astype(proj_ref.dtype)


def _gla_in(h, w_in_t, layer, w_a_t, w_b, b_b, qkv_cols, r_start, r_cols):
    m, d = h.shape
    dk = w_b.shape[1]
    n_qkv = qkv_cols // GLA_IN_TILE
    n_steps = n_qkv + r_cols // GLA_IN_TILE
    n_lg = dk // GLA_LG_TILE
    assert n_qkv * GLA_IN_TILE == qkv_cols and r_cols % GLA_IN_TILE == 0 and m % FFN_ROW_TILE == 0
    assert n_lg * GLA_LG_TILE == dk and n_lg <= n_steps and r_start % 8 == 0

    def w_rows(i, j):
        row = jnp.where(j < n_qkv, j * GLA_IN_TILE, r_start + (j - n_qkv) * GLA_IN_TILE)
        return (layer, pl.multiple_of(row, 8), 0)

    def lg_block(i, j):
        return jnp.minimum(j, n_lg - 1)

    return pl.pallas_call(
        _gla_in_kernel,
        grid=(m // FFN_ROW_TILE, n_steps),
        in_specs=[
            pl.BlockSpec((FFN_ROW_TILE, d), lambda i, j: (i, 0)),
            pl.BlockSpec((pl.Element(1), pl.Element(GLA_IN_TILE), pl.Element(d)), w_rows),
            pl.BlockSpec((V7X_LANES, d), lambda i, j: (0, 0)),
            pl.BlockSpec((V7X_LANES, GLA_LG_TILE), lambda i, j: (0, lg_block(i, j))),
            pl.BlockSpec((1, GLA_LG_TILE), lambda i, j: (0, lg_block(i, j))),
        ],
        out_specs=[
            pl.BlockSpec((FFN_ROW_TILE, GLA_IN_TILE), lambda i, j: (i, j)),
            pl.BlockSpec((FFN_ROW_TILE, GLA_LG_TILE), lambda i, j: (i, lg_block(i, j))),
        ],
        out_shape=[
            jax.ShapeDtypeStruct((m, qkv_cols + r_cols), BF16),
            jax.ShapeDtypeStruct((m, dk), F32),
        ],
        scratch_shapes=[pltpu.VMEM((FFN_ROW_TILE, V7X_LANES), BF16)],
        compiler_params=_params("parallel", "arbitrary"),
        name="gla_in",
    )(h, w_in_t, w_a_t, w_b, b_b.reshape(1, dk))


def _gla_level_masks(chunk):
    i = np.arange(chunk)[:, None]
    j = np.arange(chunk)[None, :]
    masks = []
    for lev in range(chunk.bit_length() - 1):
        same_block = (i >> (lev + 1)) == (j >> (lev + 1))
        masks.append(same_block & (((i >> lev) & 1) == 1) & (((j >> lev) & 1) == 0))
    masks.append(i == j)
    return np.stack(masks).astype(np.float32)


def _gla_core_kernel(q_ref, k_ref, v_ref, r_ref, lg_ref, tri_ref, lm_ref, hn_ref, o_ref, st_ref,
                     *, seq_len, q_scale):
    c = pl.program_id(2)
    chunk, dk = q_ref.shape[1], q_ref.shape[2]

    @pl.when(c == 0)
    def _():
        st_ref[...] = jnp.zeros_like(st_ref)

    row = lax.broadcasted_iota(jnp.int32, (chunk, 1), 0)
    valid = row < (seq_len - c * chunk)
    q = q_ref[0]
    k = jnp.where(valid, k_ref[0], 0)
    v = jnp.where(valid, v_ref[0], 0)
    lg = jnp.where(valid, lg_ref[0], 0.0) * LOG2E

    lg_hi = lg.astype(BF16)
    lg_lo = (lg - lg_hi.astype(F32)).astype(BF16)
    tri = tri_ref[...]
    b = _dot(tri, lg_hi) + _dot(tri, lg_lo)

    n_levels = chunk.bit_length() - 1
    a = _dot_nt(q, k) * lm_ref[n_levels]
    for lev in range(n_levels):
        s = 1 << lev
        if s == 1:
            e = jnp.where((row & 1) == 1, lg, 0.0)
        elif s == 2:
            nxt = pltpu.roll(lg, chunk - 1, 0)
            prv = pltpu.roll(lg, 1, 0)
            pos = row & 3
            e = jnp.where(pos == 0, nxt, jnp.where(pos == 2, lg, jnp.where(pos == 3, lg + prv, 0.0)))
        else:
            b3 = b.reshape(chunk // (2 * s), 2 * s, dk)
            e = (-jnp.abs(b3 - b3[:, s - 1:s, :])).reshape(chunk, dk)
        f = jnp.exp2(e.astype(BF16))
        a = a + _dot_nt(q * f, k * f) * lm_ref[lev]

    b_last = b[chunk - 1:chunk, :]
    qs = q * jnp.exp2(b).astype(BF16)
    ks = k * jnp.exp2(b_last - b).astype(BF16)
    st = st_ref[...]
    o = (_dot(a.astype(BF16), v) + _dot_nt(qs, st.astype(BF16))) * q_scale
    st_ref[...] = st * jnp.exp2(b_last) + _dot_tn(v, ks)

    r = r_ref[0]
    o_ref[0] = _rms_norm(o, hn_ref[...]).astype(BF16) * (r * jax.nn.sigmoid(r))


def _gla_core(proj, lg, head_norm, batch, seq_len):
    dk_all = lg.shape[-1]
    dk = dk_all // GLA_HEADS
    dv_all = (proj.shape[-1] - 2 * dk_all) // 2
    dv = dv_all // GLA_HEADS
    proj = proj.reshape(batch, seq_len, proj.shape[-1])
    lg = lg.reshape(batch, seq_len, dk_all)
    c = GLA_CHUNK
    tri = jnp.asarray(np.tril(np.ones((c, c), np.float32)), BF16)
    level_masks = jnp.asarray(_gla_level_masks(c))
    k_blk = dk_all // dk
    v_blk = 2 * dk_all // dv
    r_blk = (2 * dk_all + dv_all) // dv
    out = pl.pallas_call(
        functools.partial(_gla_core_kernel, seq_len=seq_len, q_scale=dk ** -0.5),
        grid=(batch, GLA_HEADS, pl.cdiv(seq_len, c)),
        in_specs=[
            pl.BlockSpec((1, c, dk), lambda b, h, i: (b, i, h)),
            pl.BlockSpec((1, c, dk), lambda b, h, i: (b, i, k_blk + h)),
            pl.BlockSpec((1, c, dv), lambda b, h, i: (b, i, v_blk + h)),
            pl.BlockSpec((1, c, dv), lambda b, h, i: (b, i, r_blk + h)),
            pl.BlockSpec((1, c, dk), lambda b, h, i: (b, i, h)),
            pl.BlockSpec((c, c), lambda b, h, i: (0, 0)),
            pl.BlockSpec(level_masks.shape, lambda b, h, i: (0, 0, 0)),
            pl.BlockSpec((1, dv), lambda b, h, i: (0, 0)),
        ],
        out_specs=pl.BlockSpec((1, c, dv), lambda b, h, i: (b, i, h)),
        out_shape=jax.ShapeDtypeStruct((batch, seq_len, dv_all), BF16),
        scratch_shapes=[pltpu.VMEM((dv, dk), F32)],
        compiler_params=_params("parallel", "parallel", "arbitrary"),
        name="gla_core",
    )(proj, proj, proj, proj, lg, tri, level_masks, head_norm.reshape(1, dv))
    return out.reshape(batch * seq_len, dv_all)


def _proj_add_kernel(x_ref, a_ref, w_ref, o_ref):
    o_ref[...] = x_ref[...] + _dot(a_ref[...], w_ref[...])


def _proj_add(x, a, w):
    m, d = x.shape
    kdim = a.shape[1]
    return pl.pallas_call(
        _proj_add_kernel,
        grid=(m // ROW_TILE,),
        in_specs=[
            pl.BlockSpec((ROW_TILE, d), lambda i: (i, 0)),
            pl.BlockSpec((ROW_TILE, kdim), lambda i: (i, 0)),
            pl.BlockSpec((kdim, d), lambda i: (0, 0)),
        ],
        out_specs=pl.BlockSpec((ROW_TILE, d), lambda i: (i, 0)),
        out_shape=jax.ShapeDtypeStruct((m, d), F32),
        compiler_params=_params("parallel"),
        name="gla_out",
    )(x, a, w)


def _pool_kernel(x_ref, halo_ref, g_ref, w_ref, b_ref, sc_ref, o_ref, *, tiles_per_seq):
    i = pl.program_id(0)
    tile, d = x_ref.shape
    gw = w_ref.shape[1]
    first = (i % tiles_per_seq) == 0
    x = x_ref[...]
    h = _rms_norm(x, g_ref[...])
    h_halo = jnp.where(first, 0.0, _rms_norm(halo_ref[...], g_ref[...]))
    ext = jnp.concatenate([h_halo, h], axis=0)
    t = (i % tiles_per_seq) * tile + lax.broadcasted_iota(jnp.int32, (tile, 1), 0)
    for grp, win in enumerate(POOL_WINDOWS):
        cols = slice(grp * gw, (grp + 1) * gw)
        acc = ext[:, cols]
        step = 1
        while step < win:
            acc = acc + pltpu.roll(acc, step, 0)
            step *= 2
        cnt = jnp.minimum(t + 1, win).astype(F32)
        pooled = acc[POOL_HALO:, :] / cnt - h[:, cols]
        y = _dot(pooled.astype(BF16), w_ref[grp]) + b_ref[grp]
        o_ref[:, cols] = x[:, cols] + y * sc_ref[:, cols]


def _pool_mixer(x, g, w, b, scale, seq_len):
    m, d = x.shape
    groups, gw, _ = w.shape
    tiles_per_seq = seq_len // ROW_TILE
    halo_blocks = ROW_TILE // POOL_HALO
    return pl.pallas_call(
        functools.partial(_pool_kernel, tiles_per_seq=tiles_per_seq),
        grid=(m // ROW_TILE,),
        in_specs=[
            pl.BlockSpec((ROW_TILE, d), lambda i: (i, 0)),
            pl.BlockSpec((POOL_HALO, d), lambda i: (jnp.maximum(i * halo_blocks - 1, 0), 0)),
            pl.BlockSpec((1, d), lambda i: (0, 0)),
            pl.BlockSpec((groups, gw, gw), lambda i: (0, 0, 0)),
            pl.BlockSpec((groups, 1, gw), lambda i: (0, 0, 0)),
            pl.BlockSpec((1, d), lambda i: (0, 0)),
        ],
        out_specs=pl.BlockSpec((ROW_TILE, d), lambda i: (i, 0)),
        out_shape=jax.ShapeDtypeStruct((m, d), F32),
        compiler_params=_params("parallel"),
        name="pool_mixer",
    )(x, x, g.reshape(1, d), w, b.reshape(groups, 1, gw), scale.reshape(1, d))


def kernel(x, meta, ffn_norm, ffn_w_gate, ffn_w_up, ffn_w_down, gla_norm, gla_w_in, gla_w_lr, gla_b_lr,
           gla_head_norm, gla_w_out, pool_norm, pool_w, pool_b, pool_scale, final_norm):
    batch, seq, d = x.shape
    seq_len = seq + N_META
    rows = batch * seq_len
    depth = ffn_norm.shape[0]
    dk_all = gla_w_lr.shape[-1]
    dv_all = gla_w_out.shape[1]
    assert rows % ROW_TILE == 0 and seq_len % ROW_TILE == 0
    meta = meta.astype(x.dtype)
    w_in_t = jnp.swapaxes(gla_w_in, 1, 2)
    ffn = functools.partial(_ffn_half, meta=meta, w_gate=ffn_w_gate, w_up=ffn_w_up, w_down=ffn_w_down,
                            rows=rows, seq_len=seq_len)

    xs = x
    for i in range(depth):
        j = i // 2
        gla_layer = i % 2 == 0
        first, last = i == 0, i == depth - 1
        out = ffn(xs, g=ffn_norm[i, 0], layer=i, half=0, assemble=first, next_g=gla_norm[j] if gla_layer else None)
        if gla_layer:
            xs, hn = out
            lr0 = 2 * dk_all + dv_all
            pad_rank = ((0, V7X_LANES - GLA_GATE_RANK), (0, 0))
            w_a_t = jnp.pad(w_in_t[j, lr0:lr0 + GLA_GATE_RANK], pad_rank)
            w_b = jnp.pad(gla_w_lr[j], pad_rank).astype(BF16)
            proj, lg = _gla_in(hn, w_in_t, j, w_a_t, w_b, gla_b_lr[j], lr0, lr0 + GLA_GATE_RANK, dv_all)
            gated = _gla_core(proj, lg, gla_head_norm[j], batch, seq_len)
            xs = _proj_add(xs, gated, gla_w_out[j].astype(BF16))
        else:
            xs = _pool_mixer(out, pool_norm[j], pool_w[j].astype(BF16), pool_b[j], pool_scale[j], seq_len)
        xs = ffn(xs, g=ffn_norm[i, 1], layer=i, half=1, strip=last, final_g=final_norm if last else None)
    return xs
```

```python
import functools
import math

import jax
import jax.numpy as jnp
import numpy as np
from jax import lax
from jax.experimental import pallas as pl
from jax.experimental.pallas import tpu as pltpu

F32 = jnp.float32
BF16 = jnp.bfloat16

EPS = 1e-6
N_META = 16
GLA_HEADS = 4
GLA_GATE_RANK = 16
GLA_GATE_NORM = 16.0
POOL_WINDOWS = (2, 4, 8, 16)
LOG2E = math.log2(math.e)

V7X_LANES = 128
V7X_BF16_SUBLANES = 16
V7X_VMEM_LIMIT_BYTES = 62 * 1024 * 1024

ROW_TILE = 688
FFN_ROW_TILE = 1376
FF_TILES = (512, 256)
FF_SUB = 256
GLA_CHUNK = 256
GLA_IN_TILE = 1024
GLA_LG_TILE = 256
POOL_HALO = 16
ROW_GROUP_UNROLL = 8


def _params(*semantics):
    return pltpu.CompilerParams(dimension_semantics=semantics,
                                vmem_limit_bytes=V7X_VMEM_LIMIT_BYTES)


def _rms_norm(x, g):
    return x * lax.rsqrt(jnp.mean(x * x, axis=-1, keepdims=True) + EPS) * g


def _dot(a, b):
    return jnp.dot(a, b, preferred_element_type=F32)


def _dot_nt(a, b):
    return lax.dot_general(a, b, (((1,), (1,)), ((), ())), preferred_element_type=F32)


def _dot_tn(a, b):
    return lax.dot_general(a, b, (((0,), (0,)), ((), ())), preferred_element_type=F32)


def _row_pieces(tile, seq_len, n_meta):
    period = math.lcm(tile, seq_len)
    phases = []
    for p in range(period // tile):
        pieces, r, end = [], p * tile, (p + 1) * tile
        while r < end:
            b, t = divmod(r, seq_len)
            if t < n_meta:
                n = min(n_meta - t, end - r)
                pieces.append(("meta", b, t, r - p * tile, n))
            else:
                n = min(seq_len - t, end - r)
                pieces.append(("seq", b, t - n_meta, r - p * tile, n))
            r += n
        phases.append(pieces)
    return phases, period // seq_len


def _tile_dma(action, to_vmem, t, slot, acc_ref, sems, main_hbm, meta_hbm, layout):
    tile = acc_ref.shape[1]

    def run(hbm, vmem, sem):
        copy = pltpu.make_async_copy(hbm, vmem, sem) if to_vmem else pltpu.make_async_copy(vmem, hbm, sem)
        if action == "start":
            copy.start()
        else:
            copy.wait()

    if layout is None:
        run(main_hbm.at[pl.ds(t * tile, tile)], acc_ref.at[slot], sems.at[slot, 0])
        return
    phases, batches = layout
    for p, pieces in enumerate(phases):
        @pl.when(t % len(phases) == p)
        def _():
            base = (t // len(phases)) * batches
            for idx, (kind, b, src, dst, n) in enumerate(pieces):
                vmem = acc_ref.at[slot, pl.ds(dst, n)]
                if kind == "seq":
                    run(main_hbm.at[base + b, pl.ds(src, n)], vmem, sems.at[slot, idx])
                elif to_vmem:
                    run(meta_hbm.at[pl.ds(src, n)], vmem, sems.at[slot, idx])


def _ffn_kernel(*refs, n_row_tiles, n_steps, in_layout, out_layout, final_norm, emit_h):
    if emit_h:
        (x_hbm, meta_hbm, g_ref, wg_ref, wu_ref, wd_ref, g2_ref, o_hbm, hn_hbm,
         acc_ref, h_ref, in_sem, out_sem, hn_sem) = refs
    else:
        (x_hbm, meta_hbm, g_ref, wg_ref, wu_ref, wd_ref, g2_ref, o_hbm,
         acc_ref, h_ref, in_sem, out_sem) = refs
    i = pl.program_id(0)
    j = pl.program_id(1)
    last_j = n_steps - 1
    tile = acc_ref.shape[1]
    group = V7X_BF16_SUBLANES
    n_groups = tile // group
    slot = i % 2
    load = functools.partial(_tile_dma, to_vmem=True, acc_ref=acc_ref, sems=in_sem, main_hbm=x_hbm,
                             meta_hbm=meta_hbm, layout=in_layout)
    store = functools.partial(_tile_dma, to_vmem=False, acc_ref=acc_ref, sems=out_sem, main_hbm=o_hbm,
                              meta_hbm=None, layout=out_layout)

    h_base = slot * tile if emit_h else 0

    def h_rows(first_row, rows):
        return pl.ds(pl.multiple_of(first_row, group), rows)

    def hn_copy(t, s):
        return pltpu.make_async_copy(h_ref.at[pl.ds(s * tile, tile)], hn_hbm.at[pl.ds(t * tile, tile)],
                                     hn_sem.at[s])

    @pl.when((i == 0) & (j == 0))
    def _():
        load("start", t=0, slot=0)

    @pl.when(j == 0)
    def _():
        load("wait", t=i, slot=slot)
        if emit_h:
            @pl.when(i >= 2)
            def _():
                hn_copy(i - 2, slot).wait()

        def rows_in(r, carry):
            x = acc_ref[slot, h_rows(r * group, group), :]
            h_ref[h_rows(h_base + r * group, group), :] = _rms_norm(x, g_ref[...]).astype(BF16)
            return carry

        lax.fori_loop(0, n_groups, rows_in, 0, unroll=ROW_GROUP_UNROLL)

    @pl.when((j == 1) & (i + 1 < n_row_tiles))
    def _():
        @pl.when(i >= 1)
        def _():
            store("wait", t=i - 1, slot=1 - slot)
        load("start", t=i + 1, slot=1 - slot)

    for c in range(wd_ref.shape[0] // FF_SUB):
        cols = slice(c * FF_SUB, (c + 1) * FF_SUB)
        h = h_ref[h_rows(h_base, tile), :]
        gate = _dot(h, wg_ref[:, cols].astype(BF16))
        up = _dot(h, wu_ref[:, cols].astype(BF16))
        act = (0.5 * gate * jax.nn.sigmoid(gate) * up).astype(BF16)
        acc_ref[slot] += _dot(act, wd_ref[cols, :].astype(BF16))

    @pl.when(j == last_j)
    def _():
        def rows_out(r, carry):
            rows = h_rows(r * group, group)
            hn = _rms_norm(acc_ref[slot, rows, :], g2_ref[...]).astype(BF16)
            h_ref[h_rows(h_base + r * group, group), :] = hn
            return carry

        if emit_h:
            lax.fori_loop(0, n_groups, rows_out, 0, unroll=ROW_GROUP_UNROLL)
            hn_copy(i, slot).start()
        if final_norm:
            for r in range(n_groups):
                rows = slice(r * group, (r + 1) * group)
                acc_ref[slot, rows, :] = _rms_norm(acc_ref[slot, rows, :], g2_ref[...])
        store("start", t=i, slot=slot)

    @pl.when((i == n_row_tiles - 1) & (j == last_j))
    def _():
        if n_row_tiles > 1:
            store("wait", t=i - 1, slot=1 - slot)
        store("wait", t=i, slot=slot)
        if emit_h:
            if n_row_tiles > 1:
                hn_copy(i - 1, 1 - slot).wait()
            hn_copy(i, slot).wait()


def _ffn_ff_tile(d, h_slots):
    fixed = 2 * FFN_ROW_TILE * d * 4 + h_slots * FFN_ROW_TILE * d * 2
    temps = 2 * FFN_ROW_TILE * FF_SUB * 4 + FFN_ROW_TILE * FF_SUB * 2 + d * FF_SUB * 2
    for ff_tile in FF_TILES:
        if fixed + temps + 3 * 2 * d * ff_tile * 4 <= V7X_VMEM_LIMIT_BYTES:
            return ff_tile
    raise ValueError("no D_FF chunk fits VMEM")


def _ffn_half(x, meta, g, w_gate, w_up, w_down, layer, half, *, rows, seq_len, assemble=False, strip=False,
              final_g=None, next_g=None):
    d = x.shape[-1]
    dff = w_gate.shape[-1]
    n_row_tiles = rows // FFN_ROW_TILE
    h_slots = 2 if next_g is not None else 1
    ff_tile = _ffn_ff_tile(d, h_slots)
    n_ff = dff // ff_tile
    assert n_row_tiles * FFN_ROW_TILE == rows and n_ff * ff_tile == dff and n_ff >= 2
    assert FFN_ROW_TILE % V7X_BF16_SUBLANES == 0 and not (final_g is not None and next_g is not None)
    layout = _row_pieces(FFN_ROW_TILE, seq_len, N_META)
    n_sems = max(len(p) for p in layout[0])
    g2 = final_g if final_g is not None else next_g
    g2 = jnp.ones((1, d), F32) if g2 is None else g2.reshape(1, d)
    batch = rows // seq_len
    out_shape = [jax.ShapeDtypeStruct((batch, seq_len - N_META, d) if strip else (rows, d), F32)]
    out_specs = [pl.BlockSpec(memory_space=pl.ANY)]
    if next_g is not None:
        out_shape.append(jax.ShapeDtypeStruct((rows, d), BF16))
        out_specs.append(pl.BlockSpec(memory_space=pl.ANY))
    out = pl.pallas_call(
        functools.partial(_ffn_kernel, n_row_tiles=n_row_tiles, n_steps=n_ff, in_layout=layout if assemble else None,
                          out_layout=layout if strip else None, final_norm=final_g is not None,
                          emit_h=next_g is not None),
        grid=(n_row_tiles, n_ff),
        in_specs=[
            pl.BlockSpec(memory_space=pl.ANY),
            pl.BlockSpec(memory_space=pl.ANY),
            pl.BlockSpec((1, d), lambda i, j: (0, 0)),
            pl.BlockSpec((None, None, d, ff_tile), lambda i, j: (layer, half, 0, j)),
            pl.BlockSpec((None, None, d, ff_tile), lambda i, j: (layer, half, 0, j)),
            pl.BlockSpec((None, None, ff_tile, d), lambda i, j: (layer, half, j, 0)),
            pl.BlockSpec((1, d), lambda i, j: (0, 0)),
        ],
        out_specs=out_specs,
        out_shape=out_shape,
        scratch_shapes=[
            pltpu.VMEM((2, FFN_ROW_TILE, d), F32),
            pltpu.VMEM((h_slots * FFN_ROW_TILE, d), BF16),
            pltpu.SemaphoreType.DMA((2, n_sems)),
            pltpu.SemaphoreType.DMA((2, n_sems)),
        ] + ([pltpu.SemaphoreType.DMA((2,))] if next_g is not None else []),
        compiler_params=_params("arbitrary", "arbitrary"),
        name="ffn_half",
    )(x, meta, g.reshape(1, d), w_gate, w_up, w_down, g2)
    return out if next_g is not None else out[0]


def _gla_in_kernel(h_ref, wt_ref, wa_ref, wb_ref, bb_ref, proj_ref, lg_ref, low_ref):
    @pl.when(pl.program_id(1) == 0)
    def _():
        low_ref[...] = _dot_nt(h_ref[...], wa_ref[...].astype(BF16)).astype(BF16)

    z = _dot(low_ref[...], wb_ref[...]) + bb_ref[...]
    log_sig = jnp.minimum(z, 0.0) - jnp.log(1.0 + jnp.exp(-jnp.abs(z)))
    lg_ref[...] = log_sig / GLA_GATE_NORM
    proj_ref[...] = _dot_nt(h_ref[...], wt_ref[0].astype(BF16)).astype(proj_ref.dtype)


def _gla_in(h, w_in_t, layer, w_a_t, w_b, b_b, qkv_cols, r_start, r_cols):
    m, d = h.shape
    dk = w_b.shape[1]
    n_qkv = qkv_cols // GLA_IN_TILE
    n_steps = n_qkv + r_cols // GLA_IN_TILE
    n_lg = dk // GLA_LG_TILE
    assert n_qkv * GLA_IN_TILE == qkv_cols and r_cols % GLA_IN_TILE == 0 and m % FFN_ROW_TILE == 0
    assert n_lg * GLA_LG_TILE == dk and n_lg <= n_steps and r_start % 8 == 0

    def w_rows(i, j):
        row = jnp.where(j < n_qkv, j * GLA_IN_TILE, r_start + (j - n_qkv) * GLA_IN_TILE)
        return (layer, pl.multiple_of(row, 8), 0)

    def lg_block(i, j):
        return jnp.minimum(j, n_lg - 1)

    return pl.pallas_call(
        _gla_in_kernel,
        grid=(m // FFN_ROW_TILE, n_steps),
        in_specs=[
            pl.BlockSpec((FFN_ROW_TILE, d), lambda i, j: (i, 0)),
            pl.BlockSpec((pl.Element(1), pl.Element(GLA_IN_TILE), pl.Element(d)), w_rows),
            pl.BlockSpec((V7X_LANES, d), lambda i, j: (0, 0)),
            pl.BlockSpec((V7X_LANES, GLA_LG_TILE), lambda i, j: (0, lg_block(i, j))),
            pl.BlockSpec((1, GLA_LG_TILE), lambda i, j: (0, lg_block(i, j))),
        ],
        out_specs=[
            pl.BlockSpec((FFN_ROW_TILE, GLA_IN_TILE), lambda i, j: (i, j)),
            pl.BlockSpec((FFN_ROW_TILE, GLA_LG_TILE), lambda i, j: (i, lg_block(i, j))),
        ],
        out_shape=[
            jax.ShapeDtypeStruct((m, qkv_cols + r_cols), BF16),
            jax.ShapeDtypeStruct((m, dk), F32),
        ],
        scratch_shapes=[pltpu.VMEM((FFN_ROW_TILE, V7X_LANES), BF16)],
        compiler_params=_params("parallel", "arbitrary"),
        name="gla_in",
    )(h, w_in_t, w_a_t, w_b, b_b.reshape(1, dk))


def _gla_level_masks(chunk):
    i = np.arange(chunk)[:, None]
    j = np.arange(chunk)[None, :]
    masks = []
    for lev in range(chunk.bit_length() - 1):
        same_block = (i >> (lev + 1)) == (j >> (lev + 1))
        masks.append(same_block & (((i >> lev) & 1) == 1) & (((j >> lev) & 1) == 0))
    masks.append(i == j)
    return np.stack(masks).astype(np.float32)


def _gla_core_kernel(q_ref, k_ref, v_ref, r_ref, lg_ref, tri_ref, lm_ref, hn_ref, o_ref, st_ref,
                     *, seq_len, q_scale):
    c = pl.program_id(2)
    chunk, dk = q_ref.shape[1], q_ref.shape[2]

    @pl.when(c == 0)
    def _():
        st_ref[...] = jnp.zeros_like(st_ref)

    row = lax.broadcasted_iota(jnp.int32, (chunk, 1), 0)
    valid = row < (seq_len - c * chunk)
    q = q_ref[0]
    k = jnp.where(valid, k_ref[0], 0)
    v = jnp.where(valid, v_ref[0], 0)
    lg = jnp.where(valid, lg_ref[0], 0.0) * LOG2E

    lg_hi = lg.astype(BF16)
    lg_lo = (lg - lg_hi.astype(F32)).astype(BF16)
    tri = tri_ref[...]
    b = _dot(tri, lg_hi) + _dot(tri, lg_lo)

    n_levels = chunk.bit_length() - 1
    a = _dot_nt(q, k) * lm_ref[n_levels]
    for lev in range(n_levels):
        s = 1 << lev
        if s == 1:
            e = jnp.where((row & 1) == 1, lg, 0.0)
        elif s == 2:
            nxt = pltpu.roll(lg, chunk - 1, 0)
            prv = pltpu.roll(lg, 1, 0)
            pos = row & 3
            e = jnp.where(pos == 0, nxt, jnp.where(pos == 2, lg, jnp.where(pos == 3, lg + prv, 0.0)))
        else:
            b3 = b.reshape(chunk // (2 * s), 2 * s, dk)
            e = (-jnp.abs(b3 - b3[:, s - 1:s, :])).reshape(chunk, dk)
        f = jnp.exp2(e).astype(BF16)
        a = a + _dot_nt(q * f, k * f) * lm_ref[lev]

    b_last = b[chunk - 1:chunk, :]
    qs = q * jnp.exp2(b).astype(BF16)
    ks = k * jnp.exp2(b_last - b).astype(BF16)
    st = st_ref[...]
    o = (_dot(a.astype(BF16), v) + _dot_nt(qs, st.astype(BF16))) * q_scale
    st_ref[...] = st * jnp.exp2(b_last) + _dot_tn(v, ks)

    o = _rms_norm(o, hn_ref[...])
    r = r_ref[0].astype(F32)
    o_ref[0] = (o * (r * jax.nn.sigmoid(r))).astype(o_ref.dtype)


def _gla_core(proj, lg, head_norm, batch, seq_len):
    dk_all = lg.shape[-1]
    dk = dk_all // GLA_HEADS
    dv_all = (proj.shape[-1] - 2 * dk_all) // 2
    dv = dv_all // GLA_HEADS
    proj = proj.reshape(batch, seq_len, proj.shape[-1])
    lg = lg.reshape(batch, seq_len, dk_all)
    c = GLA_CHUNK
    tri = jnp.asarray(np.tril(np.ones((c, c), np.float32)), BF16)
    level_masks = jnp.asarray(_gla_level_masks(c))
    k_blk = dk_all // dk
    v_blk = 2 * dk_all // dv
    r_blk = (2 * dk_all + dv_all) // dv
    out = pl.pallas_call(
        functools.partial(_gla_core_kernel, seq_len=seq_len, q_scale=dk ** -0.5),
        grid=(batch, GLA_HEADS, pl.cdiv(seq_len, c)),
        in_specs=[
            pl.BlockSpec((1, c, dk), lambda b, h, i: (b, i, h)),
            pl.BlockSpec((1, c, dk), lambda b, h, i: (b, i, k_blk + h)),
            pl.BlockSpec((1, c, dv), lambda b, h, i: (b, i, v_blk + h)),
            pl.BlockSpec((1, c, dv), lambda b, h, i: (b, i, r_blk + h)),
            pl.BlockSpec((1, c, dk), lambda b, h, i: (b, i, h)),
            pl.BlockSpec((c, c), lambda b, h, i: (0, 0)),
            pl.BlockSpec(level_masks.shape, lambda b, h, i: (0, 0, 0)),
            pl.BlockSpec((1, dv), lambda b, h, i: (0, 0)),
        ],
        out_specs=pl.BlockSpec((1, c, dv), lambda b, h, i: (b, i, h)),
        out_shape=jax.ShapeDtypeStruct((batch, seq_len, dv_all), BF16),
        scratch_shapes=[pltpu.VMEM((dv, dk), F32)],
        compiler_params=_params("parallel", "parallel", "arbitrary"),
        name="gla_core",
    )(proj, proj, proj, proj, lg, tri, level_masks, head_norm.reshape(1, dv))
    return out.reshape(batch * seq_len, dv_all)


def _proj_add_kernel(x_ref, a_ref, w_ref, o_ref):
    o_ref[...] = x_ref[...] + _dot(a_ref[...], w_ref[...])


def _proj_add(x, a, w):
    m, d = x.shape
    kdim = a.shape[1]
    return pl.pallas_call(
        _proj_add_kernel,
        grid=(m // ROW_TILE,),
        in_specs=[
            pl.BlockSpec((ROW_TILE, d), lambda i: (i, 0)),
            pl.BlockSpec((ROW_TILE, kdim), lambda i: (i, 0)),
            pl.BlockSpec((kdim, d), lambda i: (0, 0)),
        ],
        out_specs=pl.BlockSpec((ROW_TILE, d), lambda i: (i, 0)),
        out_shape=jax.ShapeDtypeStruct((m, d), F32),
        compiler_params=_params("parallel"),
        name="gla_out",
    )(x, a, w)


def _pool_kernel(x_ref, halo_ref, g_ref, w_ref, b_ref, sc_ref, o_ref, *, tiles_per_seq):
    i = pl.program_id(0)
    tile, d = x_ref.shape
    gw = w_ref.shape[1]
    first = (i % tiles_per_seq) == 0
    x = x_ref[...]
    h = _rms_norm(x, g_ref[...])
    h_halo = jnp.where(first, 0.0, _rms_norm(halo_ref[...], g_ref[...]))
    ext = jnp.concatenate([h_halo, h], axis=0)
    t = (i % tiles_per_seq) * tile + lax.broadcasted_iota(jnp.int32, (tile, 1), 0)
    for grp, win in enumerate(POOL_WINDOWS):
        cols = slice(grp * gw, (grp + 1) * gw)
        acc = ext[:, cols]
        step = 1
        while step < win:
            acc = acc + pltpu.roll(acc, step, 0)
            step *= 2
        cnt = jnp.minimum(t + 1, win).astype(F32)
        pooled = acc[POOL_HALO:, :] / cnt - h[:, cols]
        y = _dot(pooled.astype(BF16), w_ref[grp]) + b_ref[grp]
        o_ref[:, cols] = x[:, cols] + y * sc_ref[:, cols]


def _pool_mixer(x, g, w, b, scale, seq_len):
    m, d = x.shape
    groups, gw, _ = w.shape
    tiles_per_seq = seq_len // ROW_TILE
    halo_blocks = ROW_TILE // POOL_HALO
    return pl.pallas_call(
        functools.partial(_pool_kernel, tiles_per_seq=tiles_per_seq),
        grid=(m // ROW_TILE,),
        in_specs=[
            pl.BlockSpec((ROW_TILE, d), lambda i: (i, 0)),
            pl.BlockSpec((POOL_HALO, d), lambda i: (jnp.maximum(i * halo_blocks - 1, 0), 0)),
            pl.BlockSpec((1, d), lambda i: (0, 0)),
            pl.BlockSpec((groups, gw, gw), lambda i: (0, 0, 0)),
            pl.BlockSpec((groups, 1, gw), lambda i: (0, 0, 0)),
            pl.BlockSpec((1, d), lambda i: (0, 0)),
        ],
        out_specs=pl.BlockSpec((ROW_TILE, d), lambda i: (i, 0)),
        out_shape=jax.ShapeDtypeStruct((m, d), F32),
        compiler_params=_params("parallel"),
        name="pool_mixer",
    )(x, x, g.reshape(1, d), w, b.reshape(groups, 1, gw), scale.reshape(1, d))


def kernel(x, meta, ffn_norm, ffn_w_gate, ffn_w_up, ffn_w_down, gla_norm, gla_w_in, gla_w_lr, gla_b_lr,
           gla_head_norm, gla_w_out, pool_norm, pool_w, pool_b, pool_scale, final_norm):
    batch, seq, d = x.shape
    seq_len = seq + N_META
    rows = batch * seq_len
    depth = ffn_norm.shape[0]
    dk_all = gla_w_lr.shape[-1]
    dv_all = gla_w_out.shape[1]
    assert rows % ROW_TILE == 0 and seq_len % ROW_TILE == 0
    meta = meta.astype(x.dtype)
    w_in_t = jnp.swapaxes(gla_w_in, 1, 2)
    ffn = functools.partial(_ffn_half, meta=meta, w_gate=ffn_w_gate, w_up=ffn_w_up, w_down=ffn_w_down,
                            rows=rows, seq_len=seq_len)

    xs = x
    for i in range(depth):
        j = i // 2
        gla_layer = i % 2 == 0
        first, last = i == 0, i == depth - 1
        out = ffn(xs, g=ffn_norm[i, 0], layer=i, half=0, assemble=first, next_g=gla_norm[j] if gla_layer else None)
        if gla_layer:
            xs, hn = out
            lr0 = 2 * dk_all + dv_all
            pad_rank = ((0, V7X_LANES - GLA_GATE_RANK), (0, 0))
            w_a_t = jnp.pad(w_in_t[j, lr0:lr0 + GLA_GATE_RANK], pad_rank)
            w_b = jnp.pad(gla_w_lr[j], pad_rank).astype(BF16)
            proj, lg = _gla_in(hn, w_in_t, j, w_a_t, w_b, gla_b_lr[j], lr0, lr0 + GLA_GATE_RANK, dv_all)
            gated = _gla_core(proj, lg, gla_head_norm[j], batch, seq_len)
            xs = _proj_add(xs, gated, gla_w_out[j].astype(BF16))
        else:
            xs = _pool_mixer(out, pool_norm[j], pool_w[j].astype(BF16), pool_b[j], pool_scale[j], seq_len)
        xs = ffn(xs, g=ffn_norm[i, 1], layer=i, half=1, strip=last, final_g=final_norm if last else None)
    return xs
```

```python
import functools
import math

import jax
import jax.numpy as jnp
import numpy as np
from jax import lax
from jax.experimental import pallas as pl
from jax.experimental.pallas import tpu as pltpu

F32 = jnp.float32
BF16 = jnp.bfloat16

EPS = 1e-6
N_META = 16
GLA_HEADS = 4
GLA_GATE_RANK = 16
GLA_GATE_NORM = 16.0
POOL_WINDOWS = (2, 4, 8, 16)
LOG2E = math.log2(math.e)

V7X_LANES = 128
V7X_BF16_SUBLANES = 16
V7X_VMEM_LIMIT_BYTES = 62 * 1024 * 1024

ROW_TILE = 688
FFN_ROW_TILE = 1376
FF_TILES = (512, 256)
FF_SUB = 256
GLA_CHUNK = 256
GLA_IN_TILE = 1024
GLA_LG_TILE = 256
POOL_HALO = 16
ROW_GROUP_UNROLL = 8


def _params(*semantics):
    return pltpu.CompilerParams(dimension_semantics=semantics,
                                vmem_limit_bytes=V7X_VMEM_LIMIT_BYTES)


def _rms_norm(x, g):
    return x * lax.rsqrt(jnp.mean(x * x, axis=-1, keepdims=True) + EPS) * g


def _dot(a, b):
    return jnp.dot(a, b, preferred_element_type=F32)


def _dot_nt(a, b):
    return lax.dot_general(a, b, (((1,), (1,)), ((), ())), preferred_element_type=F32)


def _dot_tn(a, b):
    return lax.dot_general(a, b, (((0,), (0,)), ((), ())), preferred_element_type=F32)


def _row_pieces(tile, seq_len, n_meta):
    period = math.lcm(tile, seq_len)
    phases = []
    for p in range(period // tile):
        pieces, r, end = [], p * tile, (p + 1) * tile
        while r < end:
            b, t = divmod(r, seq_len)
            if t < n_meta:
                n = min(n_meta - t, end - r)
                pieces.append(("meta", b, t, r - p * tile, n))
            else:
                n = min(seq_len - t, end - r)
                pieces.append(("seq", b, t - n_meta, r - p * tile, n))
            r += n
        phases.append(pieces)
    return phases, period // seq_len


def _tile_dma(action, to_vmem, t, slot, acc_ref, sems, main_hbm, meta_hbm, layout):
    tile = acc_ref.shape[1]

    def run(hbm, vmem, sem):
        copy = pltpu.make_async_copy(hbm, vmem, sem) if to_vmem else pltpu.make_async_copy(vmem, hbm, sem)
        if action == "start":
            copy.start()
        else:
            copy.wait()

    if layout is None:
        run(main_hbm.at[pl.ds(t * tile, tile)], acc_ref.at[slot], sems.at[slot, 0])
        return
    phases, batches = layout
    for p, pieces in enumerate(phases):
        @pl.when(t % len(phases) == p)
        def _():
            base = (t // len(phases)) * batches
            for idx, (kind, b, src, dst, n) in enumerate(pieces):
                vmem = acc_ref.at[slot, pl.ds(dst, n)]
                if kind == "seq":
                    run(main_hbm.at[base + b, pl.ds(src, n)], vmem, sems.at[slot, idx])
                elif to_vmem:
                    run(meta_hbm.at[pl.ds(src, n)], vmem, sems.at[slot, idx])


def _ffn_kernel(*refs, n_row_tiles, n_steps, in_layout, out_layout, final_norm, emit_h):
    if emit_h:
        (x_hbm, meta_hbm, g_ref, wg_ref, wu_ref, wd_ref, g2_ref, o_hbm, hn_hbm,
         acc_ref, h_ref, in_sem, out_sem, hn_sem) = refs
    else:
        (x_hbm, meta_hbm, g_ref, wg_ref, wu_ref, wd_ref, g2_ref, o_hbm,
         acc_ref, h_ref, in_sem, out_sem) = refs
    i = pl.program_id(0)
    j = pl.program_id(1)
    last_j = n_steps - 1
    tile = acc_ref.shape[1]
    group = V7X_BF16_SUBLANES
    n_groups = tile // group
    slot = i % 2
    load = functools.partial(_tile_dma, to_vmem=True, acc_ref=acc_ref, sems=in_sem, main_hbm=x_hbm,
                             meta_hbm=meta_hbm, layout=in_layout)
    store = functools.partial(_tile_dma, to_vmem=False, acc_ref=acc_ref, sems=out_sem, main_hbm=o_hbm,
                              meta_hbm=None, layout=out_layout)

    h_base = slot * tile if emit_h else 0

    def h_rows(first_row, rows):
        return pl.ds(pl.multiple_of(first_row, group), rows)

    def hn_copy(t, s):
        return pltpu.make_async_copy(h_ref.at[pl.ds(s * tile, tile)], hn_hbm.at[pl.ds(t * tile, tile)],
                                     hn_sem.at[s])

    @pl.when((i == 0) & (j == 0))
    def _():
        load("start", t=0, slot=0)

    @pl.when(j == 0)
    def _():
        load("wait", t=i, slot=slot)
        if emit_h:
            @pl.when(i >= 2)
            def _():
                hn_copy(i - 2, slot).wait()

        def rows_in(r, carry):
            x = acc_ref[slot, h_rows(r * group, group), :]
            h_ref[h_rows(h_base + r * group, group), :] = _rms_norm(x, g_ref[...]).astype(BF16)
            return carry

        lax.fori_loop(0, n_groups, rows_in, 0, unroll=ROW_GROUP_UNROLL)

    @pl.when((j == 1) & (i + 1 < n_row_tiles))
    def _():
        @pl.when(i >= 1)
        def _():
            store("wait", t=i - 1, slot=1 - slot)
        load("start", t=i + 1, slot=1 - slot)

    for c in range(wd_ref.shape[0] // FF_SUB):
        cols = slice(c * FF_SUB, (c + 1) * FF_SUB)
        h = h_ref[h_rows(h_base, tile), :]
        gate = _dot(h, wg_ref[:, cols].astype(BF16))
        up = _dot(h, wu_ref[:, cols].astype(BF16))
        act = (0.5 * gate * jax.nn.sigmoid(gate) * up).astype(BF16)
        acc_ref[slot] += _dot(act, wd_ref[cols, :].astype(BF16))

    @pl.when(j == last_j)
    def _():
        def rows_out(r, carry):
            rows = h_rows(r * group, group)
            hn = _rms_norm(acc_ref[slot, rows, :], g2_ref[...]).astype(BF16)
            h_ref[h_rows(h_base + r * group, group), :] = hn
            return carry

        if emit_h:
            lax.fori_loop(0, n_groups, rows_out, 0, unroll=ROW_GROUP_UNROLL)
            hn_copy(i, slot).start()
        if final_norm:
            for r in range(n_groups):
                rows = slice(r * group, (r + 1) * group)
                acc_ref[slot, rows, :] = _rms_norm(acc_ref[slot, rows, :], g2_ref[...])
        store("start", t=i, slot=slot)

    @pl.when((i == n_row_tiles - 1) & (j == last_j))
    def _():
        if n_row_tiles > 1:
            store("wait", t=i - 1, slot=1 - slot)
        store("wait", t=i, slot=slot)
        if emit_h:
            if n_row_tiles > 1:
                hn_copy(i - 1, 1 - slot).wait()
            hn_copy(i, slot).wait()


def _ffn_ff_tile(d, h_slots):
    fixed = 2 * FFN_ROW_TILE * d * 4 + h_slots * FFN_ROW_TILE * d * 2
    temps = 2 * FFN_ROW_TILE * FF_SUB * 4 + FFN_ROW_TILE * FF_SUB * 2 + d * FF_SUB * 2
    for ff_tile in FF_TILES:
        if fixed + temps + 3 * 2 * d * ff_tile * 4 <= V7X_VMEM_LIMIT_BYTES:
            return ff_tile
    raise ValueError("no D_FF chunk fits VMEM")


def _ffn_half(x, meta, g, w_gate, w_up, w_down, layer, half, *, rows, seq_len, assemble=False, strip=False,
              final_g=None, next_g=None):
    d = x.shape[-1]
    dff = w_gate.shape[-1]
    n_row_tiles = rows // FFN_ROW_TILE
    h_slots = 2 if next_g is not None else 1
    ff_tile = _ffn_ff_tile(d, h_slots)
    n_ff = dff // ff_tile
    assert n_row_tiles * FFN_ROW_TILE == rows and n_ff * ff_tile == dff and n_ff >= 2
    assert FFN_ROW_TILE % V7X_BF16_SUBLANES == 0 and not (final_g is not None and next_g is not None)
    layout = _row_pieces(FFN_ROW_TILE, seq_len, N_META)
    n_sems = max(len(p) for p in layout[0])
    g2 = final_g if final_g is not None else next_g
    g2 = jnp.ones((1, d), F32) if g2 is None else g2.reshape(1, d)
    batch = rows // seq_len
    out_shape = [jax.ShapeDtypeStruct((batch, seq_len - N_META, d) if strip else (rows, d), F32)]
    out_specs = [pl.BlockSpec(memory_space=pl.ANY)]
    if next_g is not None:
        out_shape.append(jax.ShapeDtypeStruct((rows, d), BF16))
        out_specs.append(pl.BlockSpec(memory_space=pl.ANY))
    out = pl.pallas_call(
        functools.partial(_ffn_kernel, n_row_tiles=n_row_tiles, n_steps=n_ff, in_layout=layout if assemble else None,
                          out_layout=layout if strip else None, final_norm=final_g is not None,
                          emit_h=next_g is not None),
        grid=(n_row_tiles, n_ff),
        in_specs=[
            pl.BlockSpec(memory_space=pl.ANY),
            pl.BlockSpec(memory_space=pl.ANY),
            pl.BlockSpec((1, d), lambda i, j: (0, 0)),
            pl.BlockSpec((None, None, d, ff_tile), lambda i, j: (layer, half, 0, j)),
            pl.BlockSpec((None, None, d, ff_tile), lambda i, j: (layer, half, 0, j)),
            pl.BlockSpec((None, None, ff_tile, d), lambda i, j: (layer, half, j, 0)),
            pl.BlockSpec((1, d), lambda i, j: (0, 0)),
        ],
        out_specs=out_specs,
        out_shape=out_shape,
        scratch_shapes=[
            pltpu.VMEM((2, FFN_ROW_TILE, d), F32),
            pltpu.VMEM((h_slots * FFN_ROW_TILE, d), BF16),
            pltpu.SemaphoreType.DMA((2, n_sems)),
            pltpu.SemaphoreType.DMA((2, n_sems)),
        ] + ([pltpu.SemaphoreType.DMA((2,))] if next_g is not None else []),
        compiler_params=_params("arbitrary", "arbitrary"),
        name="ffn_half",
    )(x, meta, g.reshape(1, d), w_gate, w_up, w_down, g2)
    return out if next_g is not None else out[0]


def _gla_in_kernel(h_ref, wt_ref, wa_ref, wb_ref, bb_ref, proj_ref, lg_ref, low_ref):
    @pl.when(pl.program_id(1) == 0)
    def _():
        low_ref[...] = _dot_nt(h_ref[...], wa_ref[...].astype(BF16)).astype(BF16)

    z = _dot(low_ref[...], wb_ref[...]) + bb_ref[...]
    log_sig = jnp.minimum(z, 0.0) - jnp.log(1.0 + jnp.exp(-jnp.abs(z)))
    lg_ref[...] = log_sig / GLA_GATE_NORM
    proj_ref[...] = _dot_nt(h_ref[...], wt_ref[0].astype(BF16)).astype(proj_ref.dtype)


def _gla_in(h, w_in_t, layer, w_a_t, w_b, b_b, qkv_cols, r_start, r_cols):
    m, d = h.shape
    dk = w_b.shape[1]
    n_qkv = qkv_cols // GLA_IN_TILE
    n_steps = n_qkv + r_cols // GLA_IN_TILE
    n_lg = dk // GLA_LG_TILE
    assert n_qkv * GLA_IN_TILE == qkv_cols and r_cols % GLA_IN_TILE == 0 and m % FFN_ROW_TILE == 0
    assert n_lg * GLA_LG_TILE == dk and n_lg <= n_steps and r_start % 8 == 0

    def w_rows(i, j):
        row = jnp.where(j < n_qkv, j * GLA_IN_TILE, r_start + (j - n_qkv) * GLA_IN_TILE)
        return (layer, pl.multiple_of(row, 8), 0)

    def lg_block(i, j):
        return jnp.minimum(j, n_lg - 1)

    return pl.pallas_call(
        _gla_in_kernel,
        grid=(m // FFN_ROW_TILE, n_steps),
        in_specs=[
            pl.BlockSpec((FFN_ROW_TILE, d), lambda i, j: (i, 0)),
            pl.BlockSpec((pl.Element(1), pl.Element(GLA_IN_TILE), pl.Element(d)), w_rows),
            pl.BlockSpec((V7X_LANES, d), lambda i, j: (0, 0)),
            pl.BlockSpec((V7X_LANES, GLA_LG_TILE), lambda i, j: (0, lg_block(i, j))),
            pl.BlockSpec((1, GLA_LG_TILE), lambda i, j: (0, lg_block(i, j))),
        ],
        out_specs=[
            pl.BlockSpec((FFN_ROW_TILE, GLA_IN_TILE), lambda i, j: (i, j)),
            pl.BlockSpec((FFN_ROW_TILE, GLA_LG_TILE), lambda i, j: (i, lg_block(i, j))),
        ],
        out_shape=[
            jax.ShapeDtypeStruct((m, qkv_cols + r_cols), BF16),
            jax.ShapeDtypeStruct((m, dk), F32),
        ],
        scratch_shapes=[pltpu.VMEM((FFN_ROW_TILE, V7X_LANES), BF16)],
        compiler_params=_params("parallel", "arbitrary"),
        name="gla_in",
    )(h, w_in_t, w_a_t, w_b, b_b.reshape(1, dk))


def _gla_level_masks(chunk):
    i = np.arange(chunk)[:, None]
    j = np.arange(chunk)[None, :]
    masks = []
    for lev in range(chunk.bit_length() - 1):
        same_block = (i >> (lev + 1)) == (j >> (lev + 1))
        masks.append(same_block & (((i >> lev) & 1) == 1) & (((j >> lev) & 1) == 0))
    masks.append(i == j)
    return np.stack(masks).astype(np.float32)


def _gla_core_kernel(q_ref, k_ref, v_ref, r_ref, lg_ref, x_ref, tri_ref, lm_ref, hn_ref, wo_ref, o_ref,
                     st_ref, gated_ref, *, seq_len, q_scale):
    c = pl.program_id(1)
    step = pl.program_id(2)
    n_heads = st_ref.shape[0]
    chunk, dk = q_ref.shape[1], q_ref.shape[2]
    dv = v_ref.shape[2]

    n_levels = chunk.bit_length() - 1
    piece = o_ref.shape[2] // n_levels

    def project(head, p):
        cols = slice(p * piece, (p + 1) * piece)
        rows = pl.ds(pl.multiple_of(head * dv, dv), dv)
        o_ref[0, :, cols] += _dot(gated_ref[...], wo_ref[rows, cols])

    def recurrence(head, project_prev):
        @pl.when(c == 0)
        def _():
            st_ref[head] = jnp.zeros(st_ref.shape[1:], F32)

        row = lax.broadcasted_iota(jnp.int32, (chunk, 1), 0)
        valid = row < (seq_len - c * chunk)
        q32 = jnp.where(valid, q_ref[0].astype(F32), 0.0)
        k32 = jnp.where(valid, k_ref[0].astype(F32), 0.0)
        q = q32.astype(BF16)
        k = k32.astype(BF16)
        v = jnp.where(valid, v_ref[0].astype(F32), 0.0).astype(BF16)
        lg = jnp.where(valid, lg_ref[0], 0.0) * LOG2E

        lg_hi = lg.astype(BF16)
        lg_lo = (lg - lg_hi.astype(F32)).astype(BF16)
        tri = tri_ref[...]
        b = _dot(tri, lg_hi) + _dot(tri, lg_lo)

        a = jnp.sum(q32 * k32, axis=-1, keepdims=True) * lm_ref[n_levels]
        for lev in range(n_levels):
            s = 1 << lev
            if s == 1:
                e = jnp.where((row & 1) == 1, lg, 0.0)
            elif s == 2:
                nxt = pltpu.roll(lg, chunk - 1, 0)
                prv = pltpu.roll(lg, 1, 0)
                pos = row & 3
                e = jnp.where(pos == 0, nxt, jnp.where(pos == 2, lg, jnp.where(pos == 3, lg + prv, 0.0)))
            else:
                b3 = b.reshape(chunk // (2 * s), 2 * s, dk)
                e = (-jnp.abs(b3 - b3[:, s - 1:s, :])).reshape(chunk, dk)
            f = jnp.exp2(e).astype(BF16)
            a = a + _dot_nt(q * f, k * f) * lm_ref[lev]
            if project_prev:
                project(head - 1, lev)

        b_last = b[chunk - 1:chunk, :]
        qs = q * jnp.exp2(b).astype(BF16)
        ks = k * jnp.exp2(b_last - b).astype(BF16)
        st = st_ref[head]
        o = (_dot(a.astype(BF16), v) + _dot_nt(qs, st.astype(BF16))) * q_scale
        st_ref[head] = st * jnp.exp2(b_last) + _dot_tn(v, ks)

        o = _rms_norm(o, hn_ref[...])
        r = r_ref[0].astype(F32)
        gated_ref[...] = (o * (r * jax.nn.sigmoid(r))).astype(BF16)

    @pl.when(step == 0)
    def _():
        o_ref[...] = x_ref[...]
        recurrence(step, False)

    @pl.when((step > 0) & (step < n_heads))
    def _():
        recurrence(step, True)

    @pl.when(step == n_heads)
    def _():
        for p in range(n_levels):
            project(step - 1, p)


def _gla_mixer(x, proj, lg, head_norm, w_out, batch, seq_len):
    d = x.shape[-1]
    dk_all = lg.shape[-1]
    dk = dk_all // GLA_HEADS
    dv_all = (proj.shape[-1] - 2 * dk_all) // 2
    dv = dv_all // GLA_HEADS
    assert d % ((GLA_CHUNK.bit_length() - 1) * V7X_LANES) == 0
    proj = proj.reshape(batch, seq_len, proj.shape[-1])
    lg = lg.reshape(batch, seq_len, dk_all)
    x = x.reshape(batch, seq_len, d)
    c = GLA_CHUNK
    tri = jnp.asarray(np.tril(np.ones((c, c), np.float32)), BF16)
    level_masks = jnp.asarray(_gla_level_masks(c))
    k_blk = dk_all // dk
    v_blk = 2 * dk_all // dv
    r_blk = (2 * dk_all + dv_all) // dv

    def head(h):
        return jnp.minimum(h, GLA_HEADS - 1)

    out = pl.pallas_call(
        functools.partial(_gla_core_kernel, seq_len=seq_len, q_scale=dk ** -0.5),
        grid=(batch, pl.cdiv(seq_len, c), GLA_HEADS + 1),
        in_specs=[
            pl.BlockSpec((1, c, dk), lambda b, i, h: (b, i, head(h))),
            pl.BlockSpec((1, c, dk), lambda b, i, h: (b, i, k_blk + head(h))),
            pl.BlockSpec((1, c, dv), lambda b, i, h: (b, i, v_blk + head(h))),
            pl.BlockSpec((1, c, dv), lambda b, i, h: (b, i, r_blk + head(h))),
            pl.BlockSpec((1, c, dk), lambda b, i, h: (b, i, head(h))),
            pl.BlockSpec((1, c, d), lambda b, i, h: (b, i, 0)),
            pl.BlockSpec((c, c), lambda b, i, h: (0, 0)),
            pl.BlockSpec(level_masks.shape, lambda b, i, h: (0, 0, 0)),
            pl.BlockSpec((1, dv), lambda b, i, h: (0, 0)),
            pl.BlockSpec((dv_all, d), lambda b, i, h: (0, 0)),
        ],
        out_specs=pl.BlockSpec((1, c, d), lambda b, i, h: (b, i, 0)),
        out_shape=jax.ShapeDtypeStruct((batch, seq_len, d), F32),
        scratch_shapes=[pltpu.VMEM((GLA_HEADS, dv, dk), F32), pltpu.VMEM((c, dv), BF16)],
        compiler_params=_params("parallel", "arbitrary", "arbitrary"),
        name="gla_core",
    )(proj, proj, proj, proj, lg, x, tri, level_masks, head_norm.reshape(1, dv), w_out)
    return out.reshape(batch * seq_len, d)


def _pool_kernel(x_ref, halo_ref, g_ref, w_ref, b_ref, sc_ref, o_ref, *, tiles_per_seq):
    i = pl.program_id(0)
    tile, d = x_ref.shape
    gw = w_ref.shape[1]
    first = (i % tiles_per_seq) == 0
    x = x_ref[...]
    h = _rms_norm(x, g_ref[...])
    h_halo = jnp.where(first, 0.0, _rms_norm(halo_ref[...], g_ref[...]))
    ext = jnp.concatenate([h_halo, h], axis=0)
    t = (i % tiles_per_seq) * tile + lax.broadcasted_iota(jnp.int32, (tile, 1), 0)
    for grp, win in enumerate(POOL_WINDOWS):
        cols = slice(grp * gw, (grp + 1) * gw)
        acc = ext[:, cols]
        step = 1
        while step < win:
            acc = acc + pltpu.roll(acc, step, 0)
            step *= 2
        cnt = jnp.minimum(t + 1, win).astype(F32)
        pooled = acc[POOL_HALO:, :] / cnt - h[:, cols]
        y = _dot(pooled.astype(BF16), w_ref[grp]) + b_ref[grp]
        o_ref[:, cols] = x[:, cols] + y * sc_ref[:, cols]


def _pool_mixer(x, g, w, b, scale, seq_len):
    m, d = x.shape
    groups, gw, _ = w.shape
    tiles_per_seq = seq_len // ROW_TILE
    halo_blocks = ROW_TILE // POOL_HALO
    return pl.pallas_call(
        functools.partial(_pool_kernel, tiles_per_seq=tiles_per_seq),
        grid=(m // ROW_TILE,),
        in_specs=[
            pl.BlockSpec((ROW_TILE, d), lambda i: (i, 0)),
            pl.BlockSpec((POOL_HALO, d), lambda i: (jnp.maximum(i * halo_blocks - 1, 0), 0)),
            pl.BlockSpec((1, d), lambda i: (0, 0)),
            pl.BlockSpec((groups, gw, gw), lambda i: (0, 0, 0)),
            pl.BlockSpec((groups, 1, gw), lambda i: (0, 0, 0)),
            pl.BlockSpec((1, d), lambda i: (0, 0)),
        ],
        out_specs=pl.BlockSpec((ROW_TILE, d), lambda i: (i, 0)),
        out_shape=jax.ShapeDtypeStruct((m, d), F32),
        compiler_params=_params("parallel"),
        name="pool_mixer",
    )(x, x, g.reshape(1, d), w, b.reshape(groups, 1, gw), scale.reshape(1, d))


def kernel(x, meta, ffn_norm, ffn_w_gate, ffn_w_up, ffn_w_down, gla_norm, gla_w_in, gla_w_lr, gla_b_lr,
           gla_head_norm, gla_w_out, pool_norm, pool_w, pool_b, pool_scale, final_norm):
    batch, seq, d = x.shape
    seq_len = seq + N_META
    rows = batch * seq_len
    depth = ffn_norm.shape[0]
    dk_all = gla_w_lr.shape[-1]
    dv_all = gla_w_out.shape[1]
    assert rows % ROW_TILE == 0 and seq_len % ROW_TILE == 0
    meta = meta.astype(x.dtype)
    w_in_t = jnp.swapaxes(gla_w_in, 1, 2)
    ffn = functools.partial(_ffn_half, meta=meta, w_gate=ffn_w_gate, w_up=ffn_w_up, w_down=ffn_w_down,
                            rows=rows, seq_len=seq_len)

    xs = x
    for i in range(depth):
        j = i // 2
        gla_layer = i % 2 == 0
        first, last = i == 0, i == depth - 1
        out = ffn(xs, g=ffn_norm[i, 0], layer=i, half=0, assemble=first, next_g=gla_norm[j] if gla_layer else None)
        if gla_layer:
            xs, hn = out
            lr0 = 2 * dk_all + dv_all
            pad_rank = ((0, V7X_LANES - GLA_GATE_RANK), (0, 0))
            w_a_t = jnp.pad(w_in_t[j, lr0:lr0 + GLA_GATE_RANK], pad_rank)
            w_b = jnp.pad(gla_w_lr[j], pad_rank).astype(BF16)
            proj, lg = _gla_in(hn, w_in_t, j, w_a_t, w_b, gla_b_lr[j], lr0, lr0 + GLA_GATE_RANK, dv_all)
            xs = _gla_mixer(xs, proj, lg, gla_head_norm[j], gla_w_out[j].astype(BF16), batch, seq_len)
        else:
            xs = _pool_mixer(out, pool_norm[j], pool_w[j].astype(BF16), pool_b[j], pool_scale[j], seq_len)
        xs = ffn(xs, g=ffn_norm[i, 1], layer=i, half=1, strip=last, final_g=final_norm if last else None)
    return xs
```

```python
import functools
import math

import jax
import jax.numpy as jnp
import numpy as np
from jax import lax
from jax.experimental import pallas as pl
from jax.experimental.pallas import tpu as pltpu

F32 = jnp.float32
BF16 = jnp.bfloat16

EPS = 1e-6
N_META = 16
GLA_HEADS = 4
GLA_GATE_RANK = 16
GLA_GATE_NORM = 16.0
POOL_WINDOWS = (2, 4, 8, 16)
LOG2E = math.log2(math.e)

V7X_LANES = 128
V7X_BF16_SUBLANES = 16
V7X_VMEM_LIMIT_BYTES = 62 * 1024 * 1024

ROW_TILE = 688
FFN_ROW_TILE = 1376
FF_TILES = (512, 256)
FF_SUB = 256
GLA_CHUNK = 256
GLA_IN_TILE = 1024
GLA_LG_TILE = 256
POOL_HALO = 16
ROW_GROUP_UNROLL = 8


def _params(*semantics):
    return pltpu.CompilerParams(dimension_semantics=semantics,
                                vmem_limit_bytes=V7X_VMEM_LIMIT_BYTES)


def _rms_norm(x, g):
    return x * lax.rsqrt(jnp.mean(x * x, axis=-1, keepdims=True) + EPS) * g


def _dot(a, b):
    return jnp.dot(a, b, preferred_element_type=F32)


def _dot_nt(a, b):
    return lax.dot_general(a, b, (((1,), (1,)), ((), ())), preferred_element_type=F32)


def _dot_tn(a, b):
    return lax.dot_general(a, b, (((0,), (0,)), ((), ())), preferred_element_type=F32)


def _row_pieces(tile, seq_len, n_meta):
    period = math.lcm(tile, seq_len)
    phases = []
    for p in range(period // tile):
        pieces, r, end = [], p * tile, (p + 1) * tile
        while r < end:
            b, t = divmod(r, seq_len)
            if t < n_meta:
                n = min(n_meta - t, end - r)
                pieces.append(("meta", b, t, r - p * tile, n))
            else:
                n = min(seq_len - t, end - r)
                pieces.append(("seq", b, t - n_meta, r - p * tile, n))
            r += n
        phases.append(pieces)
    return phases, period // seq_len


def _tile_dma(action, to_vmem, t, slot, acc_ref, sems, main_hbm, meta_hbm, layout):
    tile = acc_ref.shape[1]

    def run(hbm, vmem, sem):
        copy = pltpu.make_async_copy(hbm, vmem, sem) if to_vmem else pltpu.make_async_copy(vmem, hbm, sem)
        if action == "start":
            copy.start()
        else:
            copy.wait()

    if layout is None:
        run(main_hbm.at[pl.ds(t * tile, tile)], acc_ref.at[slot], sems.at[slot, 0])
        return
    phases, batches = layout
    for p, pieces in enumerate(phases):
        @pl.when(t % len(phases) == p)
        def _():
            base = (t // len(phases)) * batches
            for idx, (kind, b, src, dst, n) in enumerate(pieces):
                vmem = acc_ref.at[slot, pl.ds(dst, n)]
                if kind == "seq":
                    run(main_hbm.at[base + b, pl.ds(src, n)], vmem, sems.at[slot, idx])
                elif to_vmem:
                    run(meta_hbm.at[pl.ds(src, n)], vmem, sems.at[slot, idx])


def _ffn_kernel(*refs, n_row_tiles, n_steps, in_layout, out_layout, final_norm, emit_h):
    if emit_h:
        (x_hbm, meta_hbm, g_ref, wg_ref, wu_ref, wd_ref, g2_ref, o_hbm, hn_hbm,
         acc_ref, h_ref, in_sem, out_sem, hn_sem) = refs
    else:
        (x_hbm, meta_hbm, g_ref, wg_ref, wu_ref, wd_ref, g2_ref, o_hbm,
         acc_ref, h_ref, in_sem, out_sem) = refs
    i = pl.program_id(0)
    j = pl.program_id(1)
    last_j = n_steps - 1
    tile = acc_ref.shape[1]
    group = V7X_BF16_SUBLANES
    n_groups = tile // group
    slot = i % 2
    load = functools.partial(_tile_dma, to_vmem=True, acc_ref=acc_ref, sems=in_sem, main_hbm=x_hbm,
                             meta_hbm=meta_hbm, layout=in_layout)
    store = functools.partial(_tile_dma, to_vmem=False, acc_ref=acc_ref, sems=out_sem, main_hbm=o_hbm,
                              meta_hbm=None, layout=out_layout)

    h_base = slot * tile if emit_h else 0

    def h_rows(first_row, rows):
        return pl.ds(pl.multiple_of(first_row, group), rows)

    def hn_copy(t, s):
        return pltpu.make_async_copy(h_ref.at[pl.ds(s * tile, tile)], hn_hbm.at[pl.ds(t * tile, tile)],
                                     hn_sem.at[s])

    @pl.when((i == 0) & (j == 0))
    def _():
        load("start", t=0, slot=0)

    @pl.when(j == 0)
    def _():
        load("wait", t=i, slot=slot)
        if emit_h:
            @pl.when(i >= 2)
            def _():
                hn_copy(i - 2, slot).wait()

        def rows_in(r, carry):
            x = acc_ref[slot, h_rows(r * group, group), :]
            h_ref[h_rows(h_base + r * group, group), :] = _rms_norm(x, g_ref[...]).astype(BF16)
            return carry

        lax.fori_loop(0, n_groups, rows_in, 0, unroll=ROW_GROUP_UNROLL)

    @pl.when((j == 1) & (i + 1 < n_row_tiles))
    def _():
        @pl.when(i >= 1)
        def _():
            store("wait", t=i - 1, slot=1 - slot)
        load("start", t=i + 1, slot=1 - slot)

    for c in range(wd_ref.shape[0] // FF_SUB):
        cols = slice(c * FF_SUB, (c + 1) * FF_SUB)
        h = h_ref[h_rows(h_base, tile), :]
        gate = _dot(h, wg_ref[:, cols].astype(BF16))
        up = _dot(h, wu_ref[:, cols].astype(BF16))
        act = (0.5 * gate * jax.nn.sigmoid(gate) * up).astype(BF16)
        acc_ref[slot] += _dot(act, wd_ref[cols, :].astype(BF16))

    @pl.when(j == last_j)
    def _():
        def rows_out(r, carry):
            rows = h_rows(r * group, group)
            hn = _rms_norm(acc_ref[slot, rows, :], g2_ref[...]).astype(BF16)
            h_ref[h_rows(h_base + r * group, group), :] = hn
            return carry

        if emit_h:
            lax.fori_loop(0, n_groups, rows_out, 0, unroll=ROW_GROUP_UNROLL)
            hn_copy(i, slot).start()
        if final_norm:
            for r in range(n_groups):
                rows = slice(r * group, (r + 1) * group)
                acc_ref[slot, rows, :] = _rms_norm(acc_ref[slot, rows, :], g2_ref[...])
        store("start", t=i, slot=slot)

    @pl.when((i == n_row_tiles - 1) & (j == last_j))
    def _():
        if n_row_tiles > 1:
            store("wait", t=i - 1, slot=1 - slot)
        store("wait", t=i, slot=slot)
        if emit_h:
            if n_row_tiles > 1:
                hn_copy(i - 1, 1 - slot).wait()
            hn_copy(i, slot).wait()


def _ffn_ff_tile(d, h_slots):
    fixed = 2 * FFN_ROW_TILE * d * 4 + h_slots * FFN_ROW_TILE * d * 2
    temps = 2 * FFN_ROW_TILE * FF_SUB * 4 + FFN_ROW_TILE * FF_SUB * 2 + d * FF_SUB * 2
    for ff_tile in FF_TILES:
        if fixed + temps + 3 * 2 * d * ff_tile * 4 <= V7X_VMEM_LIMIT_BYTES:
            return ff_tile
    raise ValueError("no D_FF chunk fits VMEM")


def _ffn_half(x, meta, g, w_gate, w_up, w_down, layer, half, *, rows, seq_len, assemble=False, strip=False,
              final_g=None, next_g=None):
    d = x.shape[-1]
    dff = w_gate.shape[-1]
    n_row_tiles = rows // FFN_ROW_TILE
    h_slots = 2 if next_g is not None else 1
    ff_tile = _ffn_ff_tile(d, h_slots)
    n_ff = dff // ff_tile
    assert n_row_tiles * FFN_ROW_TILE == rows and n_ff * ff_tile == dff and n_ff >= 2
    assert FFN_ROW_TILE % V7X_BF16_SUBLANES == 0 and not (final_g is not None and next_g is not None)
    layout = _row_pieces(FFN_ROW_TILE, seq_len, N_META)
    n_sems = max(len(p) for p in layout[0])
    g2 = final_g if final_g is not None else next_g
    g2 = jnp.ones((1, d), F32) if g2 is None else g2.reshape(1, d)
    batch = rows // seq_len
    out_shape = [jax.ShapeDtypeStruct((batch, seq_len - N_META, d) if strip else (rows, d), F32)]
    out_specs = [pl.BlockSpec(memory_space=pl.ANY)]
    if next_g is not None:
        out_shape.append(jax.ShapeDtypeStruct((rows, d), BF16))
        out_specs.append(pl.BlockSpec(memory_space=pl.ANY))
    out = pl.pallas_call(
        functools.partial(_ffn_kernel, n_row_tiles=n_row_tiles, n_steps=n_ff, in_layout=layout if assemble else None,
                          out_layout=layout if strip else None, final_norm=final_g is not None,
                          emit_h=next_g is not None),
        grid=(n_row_tiles, n_ff),
        in_specs=[
            pl.BlockSpec(memory_space=pl.ANY),
            pl.BlockSpec(memory_space=pl.ANY),
            pl.BlockSpec((1, d), lambda i, j: (0, 0)),
            pl.BlockSpec((None, None, d, ff_tile), lambda i, j: (layer, half, 0, j)),
            pl.BlockSpec((None, None, d, ff_tile), lambda i, j: (layer, half, 0, j)),
            pl.BlockSpec((None, None, ff_tile, d), lambda i, j: (layer, half, j, 0)),
            pl.BlockSpec((1, d), lambda i, j: (0, 0)),
        ],
        out_specs=out_specs,
        out_shape=out_shape,
        scratch_shapes=[
            pltpu.VMEM((2, FFN_ROW_TILE, d), F32),
            pltpu.VMEM((h_slots * FFN_ROW_TILE, d), BF16),
            pltpu.SemaphoreType.DMA((2, n_sems)),
            pltpu.SemaphoreType.DMA((2, n_sems)),
        ] + ([pltpu.SemaphoreType.DMA((2,))] if next_g is not None else []),
        compiler_params=_params("arbitrary", "arbitrary"),
        name="ffn_half",
    )(x, meta, g.reshape(1, d), w_gate, w_up, w_down, g2)
    return out if next_g is not None else out[0]


def _gla_in_kernel(h_ref, wt_ref, wa_ref, wb_ref, bb_ref, proj_ref, lg_ref, low_ref):
    @pl.when(pl.program_id(1) == 0)
    def _():
        low_ref[...] = _dot_nt(h_ref[...], wa_ref[...].astype(BF16)).astype(BF16)

    z = _dot(low_ref[...], wb_ref[...]) + bb_ref[...]
    log_sig = jnp.minimum(z, 0.0) - jnp.log(1.0 + jnp.exp(-jnp.abs(z)))
    lg_ref[...] = log_sig / GLA_GATE_NORM
    proj_ref[...] = _dot_nt(h_ref[...], wt_ref[0].astype(BF16)).astype(proj_ref.dtype)


def _gla_in(h, w_in_t, layer, w_a_t, w_b, b_b, qkv_cols, r_start, r_cols):
    m, d = h.shape
    dk = w_b.shape[1]
    n_qkv = qkv_cols // GLA_IN_TILE
    n_steps = n_qkv + r_cols // GLA_IN_TILE
    n_lg = dk // GLA_LG_TILE
    assert n_qkv * GLA_IN_TILE == qkv_cols and r_cols % GLA_IN_TILE == 0 and m % FFN_ROW_TILE == 0
    assert n_lg * GLA_LG_TILE == dk and n_lg <= n_steps and r_start % 8 == 0

    def w_rows(i, j):
        row = jnp.where(j < n_qkv, j * GLA_IN_TILE, r_start + (j - n_qkv) * GLA_IN_TILE)
        return (layer, pl.multiple_of(row, 8), 0)

    def lg_block(i, j):
        return jnp.minimum(j, n_lg - 1)

    return pl.pallas_call(
        _gla_in_kernel,
        grid=(m // FFN_ROW_TILE, n_steps),
        in_specs=[
            pl.BlockSpec((FFN_ROW_TILE, d), lambda i, j: (i, 0)),
            pl.BlockSpec((pl.Element(1), pl.Element(GLA_IN_TILE), pl.Element(d)), w_rows),
            pl.BlockSpec((V7X_LANES, d), lambda i, j: (0, 0)),
            pl.BlockSpec((V7X_LANES, GLA_LG_TILE), lambda i, j: (0, lg_block(i, j))),
            pl.BlockSpec((1, GLA_LG_TILE), lambda i, j: (0, lg_block(i, j))),
        ],
        out_specs=[
            pl.BlockSpec((FFN_ROW_TILE, GLA_IN_TILE), lambda i, j: (i, j)),
            pl.BlockSpec((FFN_ROW_TILE, GLA_LG_TILE), lambda i, j: (i, lg_block(i, j))),
        ],
        out_shape=[
            jax.ShapeDtypeStruct((m, qkv_cols + r_cols), BF16),
            jax.ShapeDtypeStruct((m, dk), F32),
        ],
        scratch_shapes=[pltpu.VMEM((FFN_ROW_TILE, V7X_LANES), BF16)],
        compiler_params=_params("parallel", "arbitrary"),
        name="gla_in",
    )(h, w_in_t, w_a_t, w_b, b_b.reshape(1, dk))


def _gla_level_masks(chunk):
    i = np.arange(chunk)[:, None]
    j = np.arange(chunk)[None, :]
    masks = []
    for lev in range(chunk.bit_length() - 1):
        same_block = (i >> (lev + 1)) == (j >> (lev + 1))
        masks.append(same_block & (((i >> lev) & 1) == 1) & (((j >> lev) & 1) == 0))
    masks.append(i == j)
    return np.stack(masks).astype(np.float32)


def _gla_core_kernel(q_ref, k_ref, v_ref, r_ref, lg_ref, tri_ref, lm_ref, hn_ref, o_ref, st_ref,
                     *, seq_len, q_scale):
    c = pl.program_id(2)
    chunk, dk = q_ref.shape[1], q_ref.shape[2]
    n_full = seq_len // chunk

    @pl.when(c == 0)
    def _():
        st_ref[...] = jnp.zeros_like(st_ref)

    def chunk_step(partial):
        row = lax.broadcasted_iota(jnp.int32, (chunk, 1), 0)
        if partial:
            valid = row < (seq_len - c * chunk)
            q32 = jnp.where(valid, q_ref[0].astype(F32), 0.0)
            k32 = jnp.where(valid, k_ref[0].astype(F32), 0.0)
            q = q32.astype(BF16)
            k = k32.astype(BF16)
            v = jnp.where(valid, v_ref[0].astype(F32), 0.0).astype(BF16)
            lg = jnp.where(valid, lg_ref[0], 0.0) * LOG2E
        else:
            q, k, v = q_ref[0], k_ref[0], v_ref[0]
            q32 = q.astype(F32)
            k32 = k.astype(F32)
            lg = lg_ref[0] * LOG2E

        lg_hi = lg.astype(BF16)
        lg_lo = (lg - lg_hi.astype(F32)).astype(BF16)
        tri = tri_ref[...]
        b = _dot(tri, lg_hi) + _dot(tri, lg_lo)

        n_levels = chunk.bit_length() - 1
        a = jnp.sum(q32 * k32, axis=-1, keepdims=True) * lm_ref[n_levels]
        for lev in range(n_levels):
            s = 1 << lev
            if s == 1:
                e = jnp.where((row & 1) == 1, lg, 0.0)
            elif s == 2:
                nxt = pltpu.roll(lg, chunk - 1, 0)
                prv = pltpu.roll(lg, 1, 0)
                pos = row & 3
                e = jnp.where(pos == 0, nxt, jnp.where(pos == 2, lg, jnp.where(pos == 3, lg + prv, 0.0)))
            else:
                b3 = b.reshape(chunk // (2 * s), 2 * s, dk)
                e = (-jnp.abs(b3 - b3[:, s - 1:s, :])).reshape(chunk, dk)
            f = jnp.exp2(e).astype(BF16)
            a = a + _dot_nt(q * f, k * f) * lm_ref[lev]

        b_last = b[chunk - 1:chunk, :]
        qs = q * jnp.exp2(b).astype(BF16)
        ks = k * jnp.exp2(b_last - b).astype(BF16)
        st = st_ref[...]
        o = (_dot(a.astype(BF16), v) + _dot_nt(qs, st.astype(BF16))) * q_scale
        st_ref[...] = st * jnp.exp2(b_last) + _dot_tn(v, ks)

        o = _rms_norm(o, hn_ref[...])
        r = r_ref[0].astype(F32)
        o_ref[0] = (o * (r * jax.nn.sigmoid(r))).astype(o_ref.dtype)

    if n_full > 0:
        pl.when(c < n_full)(functools.partial(chunk_step, False))
    if seq_len % chunk:
        pl.when(c == n_full)(functools.partial(chunk_step, True))


def _gla_core(proj, lg, head_norm, batch, seq_len):
    dk_all = lg.shape[-1]
    dk = dk_all // GLA_HEADS
    dv_all = (proj.shape[-1] - 2 * dk_all) // 2
    dv = dv_all // GLA_HEADS
    proj = proj.reshape(batch, seq_len, proj.shape[-1])
    lg = lg.reshape(batch, seq_len, dk_all)
    c = GLA_CHUNK
    tri = jnp.asarray(np.tril(np.ones((c, c), np.float32)), BF16)
    level_masks = jnp.asarray(_gla_level_masks(c))
    k_blk = dk_all // dk
    v_blk = 2 * dk_all // dv
    r_blk = (2 * dk_all + dv_all) // dv
    out = pl.pallas_call(
        functools.partial(_gla_core_kernel, seq_len=seq_len, q_scale=dk ** -0.5),
        grid=(batch, GLA_HEADS, pl.cdiv(seq_len, c)),
        in_specs=[
            pl.BlockSpec((1, c, dk), lambda b, h, i: (b, i, h)),
            pl.BlockSpec((1, c, dk), lambda b, h, i: (b, i, k_blk + h)),
            pl.BlockSpec((1, c, dv), lambda b, h, i: (b, i, v_blk + h)),
            pl.BlockSpec((1, c, dv), lambda b, h, i: (b, i, r_blk + h)),
            pl.BlockSpec((1, c, dk), lambda b, h, i: (b, i, h)),
            pl.BlockSpec((c, c), lambda b, h, i: (0, 0)),
            pl.BlockSpec(level_masks.shape, lambda b, h, i: (0, 0, 0)),
            pl.BlockSpec((1, dv), lambda b, h, i: (0, 0)),
        ],
        out_specs=pl.BlockSpec((1, c, dv), lambda b, h, i: (b, i, h)),
        out_shape=jax.ShapeDtypeStruct((batch, seq_len, dv_all), BF16),
        scratch_shapes=[pltpu.VMEM((dv, dk), F32)],
        compiler_params=_params("parallel", "parallel", "arbitrary"),
        name="gla_core",
    )(proj, proj, proj, proj, lg, tri, level_masks, head_norm.reshape(1, dv))
    return out.reshape(batch * seq_len, dv_all)


def _proj_add_kernel(x_ref, a_ref, w_ref, o_ref):
    o_ref[...] = x_ref[...] + _dot(a_ref[...], w_ref[...])


def _proj_add(x, a, w):
    m, d = x.shape
    kdim = a.shape[1]
    return pl.pallas_call(
        _proj_add_kernel,
        grid=(m // ROW_TILE,),
        in_specs=[
            pl.BlockSpec((ROW_TILE, d), lambda i: (i, 0)),
            pl.BlockSpec((ROW_TILE, kdim), lambda i: (i, 0)),
            pl.BlockSpec((kdim, d), lambda i: (0, 0)),
        ],
        out_specs=pl.BlockSpec((ROW_TILE, d), lambda i: (i, 0)),
        out_shape=jax.ShapeDtypeStruct((m, d), F32),
        compiler_params=_params("parallel"),
        name="gla_out",
    )(x, a, w)


def _pool_kernel(x_ref, halo_ref, g_ref, w_ref, b_ref, sc_ref, o_ref, *, tiles_per_seq):
    i = pl.program_id(0)
    tile, d = x_ref.shape
    gw = w_ref.shape[1]
    first = (i % tiles_per_seq) == 0
    x = x_ref[...]
    h = _rms_norm(x, g_ref[...])
    h_halo = jnp.where(first, 0.0, _rms_norm(halo_ref[...], g_ref[...]))
    ext = jnp.concatenate([h_halo, h], axis=0)
    t = (i % tiles_per_seq) * tile + lax.broadcasted_iota(jnp.int32, (tile, 1), 0)
    for grp, win in enumerate(POOL_WINDOWS):
        cols = slice(grp * gw, (grp + 1) * gw)
        acc = ext[:, cols]
        step = 1
        while step < win:
            acc = acc + pltpu.roll(acc, step, 0)
            step *= 2
        inv_cnt = 1.0 / jnp.minimum(t + 1, win).astype(F32)
        pooled = acc[POOL_HALO:, :] * inv_cnt - h[:, cols]
        y = _dot(pooled.astype(BF16), w_ref[grp]) + b_ref[grp]
        o_ref[:, cols] = x[:, cols] + y * sc_ref[:, cols]


def _pool_mixer(x, g, w, b, scale, seq_len):
    m, d = x.shape
    groups, gw, _ = w.shape
    tiles_per_seq = seq_len // ROW_TILE
    halo_blocks = ROW_TILE // POOL_HALO
    return pl.pallas_call(
        functools.partial(_pool_kernel, tiles_per_seq=tiles_per_seq),
        grid=(m // ROW_TILE,),
        in_specs=[
            pl.BlockSpec((ROW_TILE, d), lambda i: (i, 0)),
            pl.BlockSpec((POOL_HALO, d), lambda i: (jnp.maximum(i * halo_blocks - 1, 0), 0)),
            pl.BlockSpec((1, d), lambda i: (0, 0)),
            pl.BlockSpec((groups, gw, gw), lambda i: (0, 0, 0)),
            pl.BlockSpec((groups, 1, gw), lambda i: (0, 0, 0)),
            pl.BlockSpec((1, d), lambda i: (0, 0)),
        ],
        out_specs=pl.BlockSpec((ROW_TILE, d), lambda i: (i, 0)),
        out_shape=jax.ShapeDtypeStruct((m, d), F32),
        compiler_params=_params("parallel"),
        name="pool_mixer",
    )(x, x, g.reshape(1, d), w, b.reshape(groups, 1, gw), scale.reshape(1, d))


def kernel(x, meta, ffn_norm, ffn_w_gate, ffn_w_up, ffn_w_down, gla_norm, gla_w_in, gla_w_lr, gla_b_lr,
           gla_head_norm, gla_w_out, pool_norm, pool_w, pool_b, pool_scale, final_norm):
    batch, seq, d = x.shape
    seq_len = seq + N_META
    rows = batch * seq_len
    depth = ffn_norm.shape[0]
    dk_all = gla_w_lr.shape[-1]
    dv_all = gla_w_out.shape[1]
    assert rows % ROW_TILE == 0 and seq_len % ROW_TILE == 0
    meta = meta.astype(x.dtype)
    w_in_t = jnp.swapaxes(gla_w_in, 1, 2)
    ffn = functools.partial(_ffn_half, meta=meta, w_gate=ffn_w_gate, w_up=ffn_w_up, w_down=ffn_w_down,
                            rows=rows, seq_len=seq_len)

    xs = x
    for i in range(depth):
        j = i // 2
        gla_layer = i % 2 == 0
        first, last = i == 0, i == depth - 1
        out = ffn(xs, g=ffn_norm[i, 0], layer=i, half=0, assemble=first, next_g=gla_norm[j] if gla_layer else None)
        if gla_layer:
            xs, hn = out
            lr0 = 2 * dk_all + dv_all
            pad_rank = ((0, V7X_LANES - GLA_GATE_RANK), (0, 0))
            w_a_t = jnp.pad(w_in_t[j, lr0:lr0 + GLA_GATE_RANK], pad_rank)
            w_b = jnp.pad(gla_w_lr[j], pad_rank).astype(BF16)
            proj, lg = _gla_in(hn, w_in_t, j, w_a_t, w_b, gla_b_lr[j], lr0, lr0 + GLA_GATE_RANK, dv_all)
            gated = _gla_core(proj, lg, gla_head_norm[j], batch, seq_len)
            xs = _proj_add(xs, gated, gla_w_out[j].astype(BF16))
        else:
            xs = _pool_mixer(out, pool_norm[j], pool_w[j].astype(BF16), pool_b[j], pool_scale[j], seq_len)
        xs = ffn(xs, g=ffn_norm[i, 1], layer=i, half=1, strip=last, final_g=final_norm if last else None)
    return xs
```

```python
import functools
import math

import jax
import jax.numpy as jnp
import numpy as np
from jax import lax
from jax.experimental import pallas as pl
from jax.experimental.pallas import tpu as pltpu

F32 = jnp.float32
BF16 = jnp.bfloat16

EPS = 1e-6
N_META = 16
GLA_HEADS = 4
GLA_GATE_RANK = 16
GLA_GATE_NORM = 16.0
POOL_WINDOWS = (2, 4, 8, 16)
LOG2E = math.log2(math.e)

V7X_LANES = 128
V7X_BF16_SUBLANES = 16
V7X_VMEM_LIMIT_BYTES = 62 * 1024 * 1024

ROW_TILE = 688
FFN_ROW_TILE = 1376
FF_TILES = (512, 256)
FF_SUB = 256
GLA_CHUNK = 256
GLA_IN_TILE = 1024
GLA_LG_TILE = 256
POOL_HALO = 16
ROW_GROUP_UNROLL = 8


def _params(*semantics):
    return pltpu.CompilerParams(dimension_semantics=semantics,
                                vmem_limit_bytes=V7X_VMEM_LIMIT_BYTES)


def _rms_norm(x, g):
    return x * lax.rsqrt(jnp.mean(x * x, axis=-1, keepdims=True) + EPS) * g


def _dot(a, b):
    return jnp.dot(a, b, preferred_element_type=F32)


def _dot_nt(a, b):
    return lax.dot_general(a, b, (((1,), (1,)), ((), ())), preferred_element_type=F32)


def _dot_tn(a, b):
    return lax.dot_general(a, b, (((0,), (0,)), ((), ())), preferred_element_type=F32)


def _row_pieces(tile, seq_len, n_meta):
    period = math.lcm(tile, seq_len)
    phases = []
    for p in range(period // tile):
        pieces, r, end = [], p * tile, (p + 1) * tile
        while r < end:
            b, t = divmod(r, seq_len)
            if t < n_meta:
                n = min(n_meta - t, end - r)
                pieces.append(("meta", b, t, r - p * tile, n))
            else:
                n = min(seq_len - t, end - r)
                pieces.append(("seq", b, t - n_meta, r - p * tile, n))
            r += n
        phases.append(pieces)
    return phases, period // seq_len


def _tile_dma(action, to_vmem, t, slot, acc_ref, sems, main_hbm, meta_hbm, layout):
    tile = acc_ref.shape[1]

    def run(hbm, vmem, sem):
        copy = pltpu.make_async_copy(hbm, vmem, sem) if to_vmem else pltpu.make_async_copy(vmem, hbm, sem)
        if action == "start":
            copy.start()
        else:
            copy.wait()

    if layout is None:
        run(main_hbm.at[pl.ds(t * tile, tile)], acc_ref.at[slot], sems.at[slot, 0])
        return
    phases, batches = layout
    for p, pieces in enumerate(phases):
        @pl.when(t % len(phases) == p)
        def _():
            base = (t // len(phases)) * batches
            for idx, (kind, b, src, dst, n) in enumerate(pieces):
                vmem = acc_ref.at[slot, pl.ds(dst, n)]
                if kind == "seq":
                    run(main_hbm.at[base + b, pl.ds(src, n)], vmem, sems.at[slot, idx])
                elif to_vmem:
                    run(meta_hbm.at[pl.ds(src, n)], vmem, sems.at[slot, idx])


def _ffn_kernel(*refs, n_row_tiles, n_steps, in_layout, out_layout, final_norm, emit_h):
    if emit_h:
        (x_hbm, meta_hbm, g_ref, wg_ref, wu_ref, wd_ref, g2_ref, o_hbm, hn_hbm,
         acc_ref, h_ref, in_sem, out_sem, hn_sem) = refs
    else:
        (x_hbm, meta_hbm, g_ref, wg_ref, wu_ref, wd_ref, g2_ref, o_hbm,
         acc_ref, h_ref, in_sem, out_sem) = refs
    i = pl.program_id(0)
    j = pl.program_id(1)
    last_j = n_steps - 1
    tile = acc_ref.shape[1]
    group = V7X_BF16_SUBLANES
    n_groups = tile // group
    slot = i % 2
    load = functools.partial(_tile_dma, to_vmem=True, acc_ref=acc_ref, sems=in_sem, main_hbm=x_hbm,
                             meta_hbm=meta_hbm, layout=in_layout)
    store = functools.partial(_tile_dma, to_vmem=False, acc_ref=acc_ref, sems=out_sem, main_hbm=o_hbm,
                              meta_hbm=None, layout=out_layout)

    h_base = slot * tile if emit_h else 0

    def h_rows(first_row, rows):
        return pl.ds(pl.multiple_of(first_row, group), rows)

    def hn_copy(t, s):
        return pltpu.make_async_copy(h_ref.at[pl.ds(s * tile, tile)], hn_hbm.at[pl.ds(t * tile, tile)],
                                     hn_sem.at[s])

    @pl.when((i == 0) & (j == 0))
    def _():
        load("start", t=0, slot=0)

    @pl.when(j == 0)
    def _():
        load("wait", t=i, slot=slot)
        if emit_h:
            @pl.when(i >= 2)
            def _():
                hn_copy(i - 2, slot).wait()

        def rows_in(r, carry):
            x = acc_ref[slot, h_rows(r * group, group), :]
            h_ref[h_rows(h_base + r * group, group), :] = _rms_norm(x, g_ref[...]).astype(BF16)
            return carry

        lax.fori_loop(0, n_groups, rows_in, 0, unroll=ROW_GROUP_UNROLL)

    @pl.when((j == 1) & (i + 1 < n_row_tiles))
    def _():
        @pl.when(i >= 1)
        def _():
            store("wait", t=i - 1, slot=1 - slot)
        load("start", t=i + 1, slot=1 - slot)

    for c in range(wd_ref.shape[0] // FF_SUB):
        cols = slice(c * FF_SUB, (c + 1) * FF_SUB)
        h = h_ref[h_rows(h_base, tile), :]
        gate = _dot(h, wg_ref[:, cols].astype(BF16))
        up = _dot(h, wu_ref[:, cols].astype(BF16))
        act = (0.5 * gate * jax.nn.sigmoid(gate) * up).astype(BF16)
        acc_ref[slot] += _dot(act, wd_ref[cols, :].astype(BF16))

    @pl.when(j == last_j)
    def _():
        def rows_out(r, carry):
            rows = h_rows(r * group, group)
            hn = _rms_norm(acc_ref[slot, rows, :], g2_ref[...]).astype(BF16)
            h_ref[h_rows(h_base + r * group, group), :] = hn
            return carry

        if emit_h:
            lax.fori_loop(0, n_groups, rows_out, 0, unroll=ROW_GROUP_UNROLL)
            hn_copy(i, slot).start()
        if final_norm:
            for r in range(n_groups):
                rows = slice(r * group, (r + 1) * group)
                acc_ref[slot, rows, :] = _rms_norm(acc_ref[slot, rows, :], g2_ref[...])
        store("start", t=i, slot=slot)

    @pl.when((i == n_row_tiles - 1) & (j == last_j))
    def _():
        if n_row_tiles > 1:
            store("wait", t=i - 1, slot=1 - slot)
        store("wait", t=i, slot=slot)
        if emit_h:
            if n_row_tiles > 1:
                hn_copy(i - 1, 1 - slot).wait()
            hn_copy(i, slot).wait()


def _ffn_ff_tile(d, h_slots):
    fixed = 2 * FFN_ROW_TILE * d * 4 + h_slots * FFN_ROW_TILE * d * 2
    temps = 2 * FFN_ROW_TILE * FF_SUB * 4 + FFN_ROW_TILE * FF_SUB * 2 + d * FF_SUB * 2
    for ff_tile in FF_TILES:
        if fixed + temps + 3 * 2 * d * ff_tile * 4 <= V7X_VMEM_LIMIT_BYTES:
            return ff_tile
    raise ValueError("no D_FF chunk fits VMEM")


def _ffn_half(x, meta, g, w_gate, w_up, w_down, layer, half, *, rows, seq_len, assemble=False, strip=False,
              final_g=None, next_g=None):
    d = x.shape[-1]
    dff = w_gate.shape[-1]
    n_row_tiles = rows // FFN_ROW_TILE
    h_slots = 2 if next_g is not None else 1
    ff_tile = _ffn_ff_tile(d, h_slots)
    n_ff = dff // ff_tile
    assert n_row_tiles * FFN_ROW_TILE == rows and n_ff * ff_tile == dff and n_ff >= 2
    assert FFN_ROW_TILE % V7X_BF16_SUBLANES == 0 and not (final_g is not None and next_g is not None)
    layout = _row_pieces(FFN_ROW_TILE, seq_len, N_META)
    n_sems = max(len(p) for p in layout[0])
    g2 = final_g if final_g is not None else next_g
    g2 = jnp.ones((1, d), F32) if g2 is None else g2.reshape(1, d)
    batch = rows // seq_len
    out_shape = [jax.ShapeDtypeStruct((batch, seq_len - N_META, d) if strip else (rows, d), F32)]
    out_specs = [pl.BlockSpec(memory_space=pl.ANY)]
    if next_g is not None:
        out_shape.append(jax.ShapeDtypeStruct((rows, d), BF16))
        out_specs.append(pl.BlockSpec(memory_space=pl.ANY))
    out = pl.pallas_call(
        functools.partial(_ffn_kernel, n_row_tiles=n_row_tiles, n_steps=n_ff, in_layout=layout if assemble else None,
                          out_layout=layout if strip else None, final_norm=final_g is not None,
                          emit_h=next_g is not None),
        grid=(n_row_tiles, n_ff),
        in_specs=[
            pl.BlockSpec(memory_space=pl.ANY),
            pl.BlockSpec(memory_space=pl.ANY),
            pl.BlockSpec((1, d), lambda i, j: (0, 0)),
            pl.BlockSpec((None, None, d, ff_tile), lambda i, j: (layer, half, 0, j)),
            pl.BlockSpec((None, None, d, ff_tile), lambda i, j: (layer, half, 0, j)),
            pl.BlockSpec((None, None, ff_tile, d), lambda i, j: (layer, half, j, 0)),
            pl.BlockSpec((1, d), lambda i, j: (0, 0)),
        ],
        out_specs=out_specs,
        out_shape=out_shape,
        scratch_shapes=[
            pltpu.VMEM((2, FFN_ROW_TILE, d), F32),
            pltpu.VMEM((h_slots * FFN_ROW_TILE, d), BF16),
            pltpu.SemaphoreType.DMA((2, n_sems)),
            pltpu.SemaphoreType.DMA((2, n_sems)),
        ] + ([pltpu.SemaphoreType.DMA((2,))] if next_g is not None else []),
        compiler_params=_params("arbitrary", "arbitrary"),
        name="ffn_half",
    )(x, meta, g.reshape(1, d), w_gate, w_up, w_down, g2)
    return out if next_g is not None else out[0]


def _gla_in_kernel(h_ref, wt_ref, wa_ref, wb_ref, bb_ref, proj_ref, lg_ref, low_ref):
    @pl.when(pl.program_id(1) == 0)
    def _():
        low_ref[...] = _dot_nt(h_ref[...], wa_ref[...].astype(BF16)).astype(BF16)

    z = _dot(low_ref[...], wb_ref[...]) + bb_ref[...]
    log_sig = jnp.minimum(z, 0.0) - jnp.log(1.0 + jnp.exp(-jnp.abs(z)))
    lg_ref[...] = log_sig / GLA_GATE_NORM
    proj_ref[...] = _dot_nt(h_ref[...], wt_ref[0].astype(BF16)).astype(proj_ref.dtype)


def _gla_in(h, w_in_t, layer, w_a_t, w_b, b_b, qkv_cols, r_start, r_cols):
    m, d = h.shape
    dk = w_b.shape[1]
    n_qkv = qkv_cols // GLA_IN_TILE
    n_steps = n_qkv + r_cols // GLA_IN_TILE
    n_lg = dk // GLA_LG_TILE
    assert n_qkv * GLA_IN_TILE == qkv_cols and r_cols % GLA_IN_TILE == 0 and m % FFN_ROW_TILE == 0
    assert n_lg * GLA_LG_TILE == dk and n_lg <= n_steps and r_start % 8 == 0

    def w_rows(i, j):
        row = jnp.where(j < n_qkv, j * GLA_IN_TILE, r_start + (j - n_qkv) * GLA_IN_TILE)
        return (layer, pl.multiple_of(row, 8), 0)

    def lg_block(i, j):
        return jnp.minimum(j, n_lg - 1)

    return pl.pallas_call(
        _gla_in_kernel,
        grid=(m // FFN_ROW_TILE, n_steps),
        in_specs=[
            pl.BlockSpec((FFN_ROW_TILE, d), lambda i, j: (i, 0)),
            pl.BlockSpec((pl.Element(1), pl.Element(GLA_IN_TILE), pl.Element(d)), w_rows),
            pl.BlockSpec((V7X_LANES, d), lambda i, j: (0, 0)),
            pl.BlockSpec((V7X_LANES, GLA_LG_TILE), lambda i, j: (0, lg_block(i, j))),
            pl.BlockSpec((1, GLA_LG_TILE), lambda i, j: (0, lg_block(i, j))),
        ],
        out_specs=[
            pl.BlockSpec((FFN_ROW_TILE, GLA_IN_TILE), lambda i, j: (i, j)),
            pl.BlockSpec((FFN_ROW_TILE, GLA_LG_TILE), lambda i, j: (i, lg_block(i, j))),
        ],
        out_shape=[
            jax.ShapeDtypeStruct((m, qkv_cols + r_cols), BF16),
            jax.ShapeDtypeStruct((m, dk), F32),
        ],
        scratch_shapes=[pltpu.VMEM((FFN_ROW_TILE, V7X_LANES), BF16)],
        compiler_params=_params("parallel", "arbitrary"),
        name="gla_in",
    )(h, w_in_t, w_a_t, w_b, b_b.reshape(1, dk))


def _gla_level_masks(chunk):
    i = np.arange(chunk)[:, None]
    j = np.arange(chunk)[None, :]
    masks = []
    for lev in range(chunk.bit_length() - 1):
        same_block = (i >> (lev + 1)) == (j >> (lev + 1))
        masks.append(same_block & (((i >> lev) & 1) == 1) & (((j >> lev) & 1) == 0))
    masks.append(i == j)
    return np.stack(masks).astype(np.float32)


def _gla_core_kernel(q_ref, k_ref, v_ref, r_ref, lg_ref, tri_ref, lm_ref, hn_ref, o_ref, st_ref,
                     *, seq_len, q_scale):
    c = pl.program_id(1)
    n_heads = st_ref.shape[0]
    chunk = q_ref.shape[1]
    dk = q_ref.shape[2] // n_heads
    dv = v_ref.shape[2] // n_heads
    n_full = seq_len // chunk

    @pl.when(c == 0)
    def _():
        st_ref[...] = jnp.zeros_like(st_ref)

    def head_step(head, partial):
        kcols = slice(head * dk, (head + 1) * dk)
        vcols = slice(head * dv, (head + 1) * dv)
        row = lax.broadcasted_iota(jnp.int32, (chunk, 1), 0)
        if partial:
            valid = row < (seq_len - c * chunk)
            q32 = jnp.where(valid, q_ref[0, :, kcols].astype(F32), 0.0)
            k32 = jnp.where(valid, k_ref[0, :, kcols].astype(F32), 0.0)
            q = q32.astype(BF16)
            k = k32.astype(BF16)
            v = jnp.where(valid, v_ref[0, :, vcols].astype(F32), 0.0).astype(BF16)
            lg = jnp.where(valid, lg_ref[0, :, kcols], 0.0) * LOG2E
        else:
            q, k, v = q_ref[0, :, kcols], k_ref[0, :, kcols], v_ref[0, :, vcols]
            q32 = q.astype(F32)
            k32 = k.astype(F32)
            lg = lg_ref[0, :, kcols] * LOG2E

        lg_hi = lg.astype(BF16)
        lg_lo = (lg - lg_hi.astype(F32)).astype(BF16)
        tri = tri_ref[...]
        b = _dot(tri, lg_hi) + _dot(tri, lg_lo)

        n_levels = chunk.bit_length() - 1
        a = jnp.sum(q32 * k32, axis=-1, keepdims=True) * lm_ref[n_levels]
        for lev in range(n_levels):
            s = 1 << lev
            if s == 1:
                e = jnp.where((row & 1) == 1, lg, 0.0)
            elif s == 2:
                nxt = pltpu.roll(lg, chunk - 1, 0)
                prv = pltpu.roll(lg, 1, 0)
                pos = row & 3
                e = jnp.where(pos == 0, nxt, jnp.where(pos == 2, lg, jnp.where(pos == 3, lg + prv, 0.0)))
            else:
                b3 = b.reshape(chunk // (2 * s), 2 * s, dk)
                e = (-jnp.abs(b3 - b3[:, s - 1:s, :])).reshape(chunk, dk)
            f = jnp.exp2(e).astype(BF16)
            a = a + _dot_nt(q * f, k * f) * lm_ref[lev]

        b_last = b[chunk - 1:chunk, :]
        qs = q * jnp.exp2(b).astype(BF16)
        ks = k * jnp.exp2(b_last - b).astype(BF16)
        st = st_ref[head]
        o = (_dot(a.astype(BF16), v) + _dot_nt(qs, st.astype(BF16))) * q_scale
        st_ref[head] = st * jnp.exp2(b_last) + _dot_tn(v, ks)

        o = _rms_norm(o, hn_ref[...])
        r = r_ref[0, :, vcols].astype(F32)
        o_ref[0, :, vcols] = (o * (r * jax.nn.sigmoid(r))).astype(o_ref.dtype)

    def chunk_step(partial):
        for head in range(n_heads):
            head_step(head, partial)

    if n_full > 0:
        pl.when(c < n_full)(functools.partial(chunk_step, False))
    if seq_len % chunk:
        pl.when(c == n_full)(functools.partial(chunk_step, True))


def _gla_core(proj, lg, head_norm, batch, seq_len):
    dk_all = lg.shape[-1]
    dk = dk_all // GLA_HEADS
    dv_all = (proj.shape[-1] - 2 * dk_all) // 2
    dv = dv_all // GLA_HEADS
    proj = proj.reshape(batch, seq_len, proj.shape[-1])
    lg = lg.reshape(batch, seq_len, dk_all)
    c = GLA_CHUNK
    tri = jnp.asarray(np.tril(np.ones((c, c), np.float32)), BF16)
    level_masks = jnp.asarray(_gla_level_masks(c))
    assert dv_all == 2 * dk_all
    out = pl.pallas_call(
        functools.partial(_gla_core_kernel, seq_len=seq_len, q_scale=dk ** -0.5),
        grid=(batch, pl.cdiv(seq_len, c)),
        in_specs=[
            pl.BlockSpec((1, c, dk_all), lambda b, i: (b, i, 0)),
            pl.BlockSpec((1, c, dk_all), lambda b, i: (b, i, 1)),
            pl.BlockSpec((1, c, dv_all), lambda b, i: (b, i, 1)),
            pl.BlockSpec((1, c, dv_all), lambda b, i: (b, i, 2)),
            pl.BlockSpec((1, c, dk_all), lambda b, i: (b, i, 0)),
            pl.BlockSpec((c, c), lambda b, i: (0, 0)),
            pl.BlockSpec(level_masks.shape, lambda b, i: (0, 0, 0)),
            pl.BlockSpec((1, dv), lambda b, i: (0, 0)),
        ],
        out_specs=pl.BlockSpec((1, c, dv_all), lambda b, i: (b, i, 0)),
        out_shape=jax.ShapeDtypeStruct((batch, seq_len, dv_all), BF16),
        scratch_shapes=[pltpu.VMEM((GLA_HEADS, dv, dk), F32)],
        compiler_params=_params("parallel", "arbitrary"),
        name="gla_core",
    )(proj, proj, proj, proj, lg, tri, level_masks, head_norm.reshape(1, dv))
    return out.reshape(batch * seq_len, dv_all)


def _proj_add_kernel(x_ref, a_ref, w_ref, o_ref):
    o_ref[...] = x_ref[...] + _dot(a_ref[...], w_ref[...])


def _proj_add(x, a, w):
    m, d = x.shape
    kdim = a.shape[1]
    return pl.pallas_call(
        _proj_add_kernel,
        grid=(m // ROW_TILE,),
        in_specs=[
            pl.BlockSpec((ROW_TILE, d), lambda i: (i, 0)),
            pl.BlockSpec((ROW_TILE, kdim), lambda i: (i, 0)),
            pl.BlockSpec((kdim, d), lambda i: (0, 0)),
        ],
        out_specs=pl.BlockSpec((ROW_TILE, d), lambda i: (i, 0)),
        out_shape=jax.ShapeDtypeStruct((m, d), F32),
        compiler_params=_params("parallel"),
        name="gla_out",
    )(x, a, w)


def _pool_kernel(x_ref, halo_ref, g_ref, w_ref, b_ref, sc_ref, o_ref, *, tiles_per_seq):
    i = pl.program_id(0)
    tile, d = x_ref.shape
    gw = w_ref.shape[1]
    first = (i % tiles_per_seq) == 0
    x = x_ref[...]
    h = _rms_norm(x, g_ref[...])
    h_halo = jnp.where(first, 0.0, _rms_norm(halo_ref[...], g_ref[...]))
    ext = jnp.concatenate([h_halo, h], axis=0)
    t = (i % tiles_per_seq) * tile + lax.broadcasted_iota(jnp.int32, (tile, 1), 0)
    for grp, win in enumerate(POOL_WINDOWS):
        cols = slice(grp * gw, (grp + 1) * gw)
        acc = ext[:, cols]
        step = 1
        while step < win:
            acc = acc + pltpu.roll(acc, step, 0)
            step *= 2
        inv_cnt = 1.0 / jnp.minimum(t + 1, win).astype(F32)
        pooled = acc[POOL_HALO:, :] * inv_cnt - h[:, cols]
        y = _dot(pooled.astype(BF16), w_ref[grp]) + b_ref[grp]
        o_ref[:, cols] = x[:, cols] + y * sc_ref[:, cols]


def _pool_mixer(x, g, w, b, scale, seq_len):
    m, d = x.shape
    groups, gw, _ = w.shape
    tiles_per_seq = seq_len // ROW_TILE
    halo_blocks = ROW_TILE // POOL_HALO
    return pl.pallas_call(
        functools.partial(_pool_kernel, tiles_per_seq=tiles_per_seq),
        grid=(m // ROW_TILE,),
        in_specs=[
            pl.BlockSpec((ROW_TILE, d), lambda i: (i, 0)),
            pl.BlockSpec((POOL_HALO, d), lambda i: (jnp.maximum(i * halo_blocks - 1, 0), 0)),
            pl.BlockSpec((1, d), lambda i: (0, 0)),
            pl.BlockSpec((groups, gw, gw), lambda i: (0, 0, 0)),
            pl.BlockSpec((groups, 1, gw), lambda i: (0, 0, 0)),
            pl.BlockSpec((1, d), lambda i: (0, 0)),
        ],
        out_specs=pl.BlockSpec((ROW_TILE, d), lambda i: (i, 0)),
        out_shape=jax.ShapeDtypeStruct((m, d), F32),
        compiler_params=_params("parallel"),
        name="pool_mixer",
    )(x, x, g.reshape(1, d), w, b.reshape(groups, 1, gw), scale.reshape(1, d))


def kernel(x, meta, ffn_norm, ffn_w_gate, ffn_w_up, ffn_w_down, gla_norm, gla_w_in, gla_w_lr, gla_b_lr,
           gla_head_norm, gla_w_out, pool_norm, pool_w, pool_b, pool_scale, final_norm):
    batch, seq, d = x.shape
    seq_len = seq + N_META
    rows = batch * seq_len
    depth = ffn_norm.shape[0]
    dk_all = gla_w_lr.shape[-1]
    dv_all = gla_w_out.shape[1]
    assert rows % ROW_TILE == 0 and seq_len % ROW_TILE == 0
    meta = meta.astype(x.dtype)
    w_in_t = jnp.swapaxes(gla_w_in, 1, 2)
    ffn = functools.partial(_ffn_half, meta=meta, w_gate=ffn_w_gate, w_up=ffn_w_up, w_down=ffn_w_down,
                            rows=rows, seq_len=seq_len)

    xs = x
    for i in range(depth):
        j = i // 2
        gla_layer = i % 2 == 0
        first, last = i == 0, i == depth - 1
        out = ffn(xs, g=ffn_norm[i, 0], layer=i, half=0, assemble=first, next_g=gla_norm[j] if gla_layer else None)
        if gla_layer:
            xs, hn = out
            lr0 = 2 * dk_all + dv_all
            pad_rank = ((0, V7X_LANES - GLA_GATE_RANK), (0, 0))
            w_a_t = jnp.pad(w_in_t[j, lr0:lr0 + GLA_GATE_RANK], pad_rank)
            w_b = jnp.pad(gla_w_lr[j], pad_rank).astype(BF16)
            proj, lg = _gla_in(hn, w_in_t, j, w_a_t, w_b, gla_b_lr[j], lr0, lr0 + GLA_GATE_RANK, dv_all)
            gated = _gla_core(proj, lg, gla_head_norm[j], batch, seq_len)
            xs = _proj_add(xs, gated, gla_w_out[j].astype(BF16))
        else:
            xs = _pool_mixer(out, pool_norm[j], pool_w[j].astype(BF16), pool_b[j], pool_scale[j], seq_len)
        xs = ffn(xs, g=ffn_norm[i, 1], layer=i, half=1, strip=last, final_g=final_norm if last else None)
    return xs
```

```python
import functools
import math

import jax
import jax.numpy as jnp
import numpy as np
from jax import lax
from jax.experimental import pallas as pl
from jax.experimental.pallas import tpu as pltpu

F32 = jnp.float32
BF16 = jnp.bfloat16

EPS = 1e-6
N_META = 16
GLA_HEADS = 4
GLA_GATE_RANK = 16
GLA_GATE_NORM = 16.0
POOL_WINDOWS = (2, 4, 8, 16)
LOG2E = math.log2(math.e)

V7X_LANES = 128
V7X_BF16_SUBLANES = 16
V7X_VMEM_LIMIT_BYTES = 62 * 1024 * 1024

ROW_TILE = 688
FFN_ROW_TILE = 1376
FF_TILES = (512, 256)
FF_SUB = 256
GLA_CHUNK = 256
GLA_IN_TILE = 1024
GLA_LG_TILE = 256
POOL_HALO = 16
ROW_GROUP_UNROLL = 8


def _params(*semantics):
    return pltpu.CompilerParams(dimension_semantics=semantics,
                                vmem_limit_bytes=V7X_VMEM_LIMIT_BYTES)


def _rms_norm(x, g):
    return x * lax.rsqrt(jnp.mean(x * x, axis=-1, keepdims=True) + EPS) * g


def _dot(a, b):
    return jnp.dot(a, b, preferred_element_type=F32)


def _dot_nt(a, b):
    return lax.dot_general(a, b, (((1,), (1,)), ((), ())), preferred_element_type=F32)


def _dot_tn(a, b):
    return lax.dot_general(a, b, (((0,), (0,)), ((), ())), preferred_element_type=F32)


def _row_pieces(tile, seq_len, n_meta):
    period = math.lcm(tile, seq_len)
    phases = []
    for p in range(period // tile):
        pieces, r, end = [], p * tile, (p + 1) * tile
        while r < end:
            b, t = divmod(r, seq_len)
            if t < n_meta:
                n = min(n_meta - t, end - r)
                pieces.append(("meta", b, t, r - p * tile, n))
            else:
                n = min(seq_len - t, end - r)
                pieces.append(("seq", b, t - n_meta, r - p * tile, n))
            r += n
        phases.append(pieces)
    return phases, period // seq_len


def _tile_dma(action, to_vmem, t, slot, acc_ref, sems, main_hbm, meta_hbm, layout):
    tile = acc_ref.shape[1]

    def run(hbm, vmem, sem):
        copy = pltpu.make_async_copy(hbm, vmem, sem) if to_vmem else pltpu.make_async_copy(vmem, hbm, sem)
        if action == "start":
            copy.start()
        else:
            copy.wait()

    if layout is None:
        run(main_hbm.at[pl.ds(t * tile, tile)], acc_ref.at[slot], sems.at[slot, 0])
        return
    phases, batches = layout
    for p, pieces in enumerate(phases):
        @pl.when(t % len(phases) == p)
        def _():
            base = (t // len(phases)) * batches
            for idx, (kind, b, src, dst, n) in enumerate(pieces):
                vmem = acc_ref.at[slot, pl.ds(dst, n)]
                if kind == "seq":
                    run(main_hbm.at[base + b, pl.ds(src, n)], vmem, sems.at[slot, idx])
                elif to_vmem:
                    run(meta_hbm.at[pl.ds(src, n)], vmem, sems.at[slot, idx])


def _ffn_kernel(*refs, n_row_tiles, n_steps, in_layout, out_layout, final_norm, emit_h):
    if emit_h:
        (x_hbm, meta_hbm, g_ref, wg_ref, wu_ref, wd_ref, g2_ref, o_hbm, hn_hbm,
         acc_ref, h_ref, in_sem, out_sem, hn_sem) = refs
    else:
        (x_hbm, meta_hbm, g_ref, wg_ref, wu_ref, wd_ref, g2_ref, o_hbm,
         acc_ref, h_ref, in_sem, out_sem) = refs
    i = pl.program_id(0)
    j = pl.program_id(1)
    last_j = n_steps - 1
    tile = acc_ref.shape[1]
    group = V7X_BF16_SUBLANES
    n_groups = tile // group
    slot = i % 2
    load = functools.partial(_tile_dma, to_vmem=True, acc_ref=acc_ref, sems=in_sem, main_hbm=x_hbm,
                             meta_hbm=meta_hbm, layout=in_layout)
    store = functools.partial(_tile_dma, to_vmem=False, acc_ref=acc_ref, sems=out_sem, main_hbm=o_hbm,
                              meta_hbm=None, layout=out_layout)

    h_base = slot * tile if emit_h else 0

    def h_rows(first_row, rows):
        return pl.ds(pl.multiple_of(first_row, group), rows)

    def hn_copy(t, s):
        return pltpu.make_async_copy(h_ref.at[pl.ds(s * tile, tile)], hn_hbm.at[pl.ds(t * tile, tile)],
                                     hn_sem.at[s])

    @pl.when((i == 0) & (j == 0))
    def _():
        load("start", t=0, slot=0)

    @pl.when(j == 0)
    def _():
        load("wait", t=i, slot=slot)
        if emit_h:
            @pl.when(i >= 2)
            def _():
                hn_copy(i - 2, slot).wait()

        def rows_in(r, carry):
            x = acc_ref[slot, h_rows(r * group, group), :]
            h_ref[h_rows(h_base + r * group, group), :] = _rms_norm(x, g_ref[...]).astype(BF16)
            return carry

        lax.fori_loop(0, n_groups, rows_in, 0, unroll=ROW_GROUP_UNROLL)

    @pl.when((j == 1) & (i + 1 < n_row_tiles))
    def _():
        @pl.when(i >= 1)
        def _():
            store("wait", t=i - 1, slot=1 - slot)
        load("start", t=i + 1, slot=1 - slot)

    for c in range(wd_ref.shape[0] // FF_SUB):
        cols = slice(c * FF_SUB, (c + 1) * FF_SUB)
        h = h_ref[h_rows(h_base, tile), :]
        gate = _dot(h, wg_ref[:, cols].astype(BF16))
        up = _dot(h, wu_ref[:, cols].astype(BF16))
        act = (0.5 * gate * jax.nn.sigmoid(gate) * up).astype(BF16)
        acc_ref[slot] += _dot(act, wd_ref[cols, :].astype(BF16))

    @pl.when(j == last_j)
    def _():
        def rows_out(r, carry):
            rows = h_rows(r * group, group)
            hn = _rms_norm(acc_ref[slot, rows, :], g2_ref[...]).astype(BF16)
            h_ref[h_rows(h_base + r * group, group), :] = hn
            return carry

        if emit_h:
            lax.fori_loop(0, n_groups, rows_out, 0, unroll=ROW_GROUP_UNROLL)
            hn_copy(i, slot).start()
        if final_norm:
            for r in range(n_groups):
                rows = slice(r * group, (r + 1) * group)
                acc_ref[slot, rows, :] = _rms_norm(acc_ref[slot, rows, :], g2_ref[...])
        store("start", t=i, slot=slot)

    @pl.when((i == n_row_tiles - 1) & (j == last_j))
    def _():
        if n_row_tiles > 1:
            store("wait", t=i - 1, slot=1 - slot)
        store("wait", t=i, slot=slot)
        if emit_h:
            if n_row_tiles > 1:
                hn_copy(i - 1, 1 - slot).wait()
            hn_copy(i, slot).wait()


def _ffn_ff_tile(d, h_slots):
    fixed = 2 * FFN_ROW_TILE * d * 4 + h_slots * FFN_ROW_TILE * d * 2
    temps = 2 * FFN_ROW_TILE * FF_SUB * 4 + FFN_ROW_TILE * FF_SUB * 2 + d * FF_SUB * 2
    for ff_tile in FF_TILES:
        if fixed + temps + 3 * 2 * d * ff_tile * 4 <= V7X_VMEM_LIMIT_BYTES:
            return ff_tile
    raise ValueError("no D_FF chunk fits VMEM")


def _ffn_half(x, meta, g, w_gate, w_up, w_down, layer, half, *, rows, seq_len, assemble=False, strip=False,
              final_g=None, next_g=None):
    d = x.shape[-1]
    dff = w_gate.shape[-1]
    n_row_tiles = rows // FFN_ROW_TILE
    h_slots = 2 if next_g is not None else 1
    ff_tile = _ffn_ff_tile(d, h_slots)
    n_ff = dff // ff_tile
    assert n_row_tiles * FFN_ROW_TILE == rows and n_ff * ff_tile == dff and n_ff >= 2
    assert FFN_ROW_TILE % V7X_BF16_SUBLANES == 0 and not (final_g is not None and next_g is not None)
    layout = _row_pieces(FFN_ROW_TILE, seq_len, N_META)
    n_sems = max(len(p) for p in layout[0])
    g2 = final_g if final_g is not None else next_g
    g2 = jnp.ones((1, d), F32) if g2 is None else g2.reshape(1, d)
    batch = rows // seq_len
    out_shape = [jax.ShapeDtypeStruct((batch, seq_len - N_META, d) if strip else (rows, d), F32)]
    out_specs = [pl.BlockSpec(memory_space=pl.ANY)]
    if next_g is not None:
        out_shape.append(jax.ShapeDtypeStruct((rows, d), BF16))
        out_specs.append(pl.BlockSpec(memory_space=pl.ANY))
    out = pl.pallas_call(
        functools.partial(_ffn_kernel, n_row_tiles=n_row_tiles, n_steps=n_ff, in_layout=layout if assemble else None,
                          out_layout=layout if strip else None, final_norm=final_g is not None,
                          emit_h=next_g is not None),
        grid=(n_row_tiles, n_ff),
        in_specs=[
            pl.BlockSpec(memory_space=pl.ANY),
            pl.BlockSpec(memory_space=pl.ANY),
            pl.BlockSpec((1, d), lambda i, j: (0, 0)),
            pl.BlockSpec((None, None, d, ff_tile), lambda i, j: (layer, half, 0, j)),
            pl.BlockSpec((None, None, d, ff_tile), lambda i, j: (layer, half, 0, j)),
            pl.BlockSpec((None, None, ff_tile, d), lambda i, j: (layer, half, j, 0)),
            pl.BlockSpec((1, d), lambda i, j: (0, 0)),
        ],
        out_specs=out_specs,
        out_shape=out_shape,
        scratch_shapes=[
            pltpu.VMEM((2, FFN_ROW_TILE, d), F32),
            pltpu.VMEM((h_slots * FFN_ROW_TILE, d), BF16),
            pltpu.SemaphoreType.DMA((2, n_sems)),
            pltpu.SemaphoreType.DMA((2, n_sems)),
        ] + ([pltpu.SemaphoreType.DMA((2,))] if next_g is not None else []),
        compiler_params=_params("arbitrary", "arbitrary"),
        name="ffn_half",
    )(x, meta, g.reshape(1, d), w_gate, w_up, w_down, g2)
    return out if next_g is not None else out[0]


def _gla_in_kernel(h_ref, wt_ref, wa_ref, wb_ref, bb_ref, proj_ref, lg_ref, low_ref):
    @pl.when(pl.program_id(1) == 0)
    def _():
        low_ref[...] = _dot_nt(h_ref[...], wa_ref[...].astype(BF16)).astype(BF16)

    z = _dot(low_ref[...], wb_ref[...]) + bb_ref[...]
    log_sig = jnp.minimum(z, 0.0) - jnp.log(1.0 + jnp.exp(-jnp.abs(z)))
    lg_ref[...] = log_sig / GLA_GATE_NORM
    proj_ref[...] = _dot_nt(h_ref[...], wt_ref[0].astype(BF16)).astype(proj_ref.dtype)


def _gla_in(h, w_in_t, layer, w_a_t, w_b, b_b, qkv_cols, r_start, r_cols):
    m, d = h.shape
    dk = w_b.shape[1]
    n_qkv = qkv_cols // GLA_IN_TILE
    n_steps = n_qkv + r_cols // GLA_IN_TILE
    n_lg = dk // GLA_LG_TILE
    assert n_qkv * GLA_IN_TILE == qkv_cols and r_cols % GLA_IN_TILE == 0 and m % FFN_ROW_TILE == 0
    assert n_lg * GLA_LG_TILE == dk and n_lg <= n_steps and r_start % 8 == 0

    def w_rows(i, j):
        row = jnp.where(j < n_qkv, j * GLA_IN_TILE, r_start + (j - n_qkv) * GLA_IN_TILE)
        return (layer, pl.multiple_of(row, 8), 0)

    def lg_block(i, j):
        return jnp.minimum(j, n_lg - 1)

    return pl.pallas_call(
        _gla_in_kernel,
        grid=(m // FFN_ROW_TILE, n_steps),
        in_specs=[
            pl.BlockSpec((FFN_ROW_TILE, d), lambda i, j: (i, 0)),
            pl.BlockSpec((pl.Element(1), pl.Element(GLA_IN_TILE), pl.Element(d)), w_rows),
            pl.BlockSpec((V7X_LANES, d), lambda i, j: (0, 0)),
            pl.BlockSpec((V7X_LANES, GLA_LG_TILE), lambda i, j: (0, lg_block(i, j))),
            pl.BlockSpec((1, GLA_LG_TILE), lambda i, j: (0, lg_block(i, j))),
        ],
        out_specs=[
            pl.BlockSpec((FFN_ROW_TILE, GLA_IN_TILE), lambda i, j: (i, j)),
            pl.BlockSpec((FFN_ROW_TILE, GLA_LG_TILE), lambda i, j: (i, lg_block(i, j))),
        ],
        out_shape=[
            jax.ShapeDtypeStruct((m, qkv_cols + r_cols), BF16),
            jax.ShapeDtypeStruct((m, dk), F32),
        ],
        scratch_shapes=[pltpu.VMEM((FFN_ROW_TILE, V7X_LANES), BF16)],
        compiler_params=_params("parallel", "arbitrary"),
        name="gla_in",
    )(h, w_in_t, w_a_t, w_b, b_b.reshape(1, dk))


def _gla_level_masks(chunk):
    i = np.arange(chunk)[:, None]
    j = np.arange(chunk)[None, :]
    masks = []
    for lev in range(chunk.bit_length() - 1):
        same_block = (i >> (lev + 1)) == (j >> (lev + 1))
        masks.append(same_block & (((i >> lev) & 1) == 1) & (((j >> lev) & 1) == 0))
    masks.append(i == j)
    return np.stack(masks).astype(np.float32)


def _gla_core_kernel(q_ref, k_ref, v_ref, r_ref, lg_ref, tri_ref, lm_ref, hn_ref, o_ref, st_ref,
                     *, seq_len, q_scale):
    c = pl.program_id(1)
    n_heads = st_ref.shape[0]
    chunk = q_ref.shape[1]
    dk = q_ref.shape[2] // n_heads
    dv = v_ref.shape[2] // n_heads
    n_full = seq_len // chunk

    @pl.when(c == 0)
    def _():
        st_ref[...] = jnp.zeros_like(st_ref)

    def head_step(head, partial):
        kcols = slice(head * dk, (head + 1) * dk)
        vcols = slice(head * dv, (head + 1) * dv)
        row = lax.broadcasted_iota(jnp.int32, (chunk, 1), 0)
        if partial:
            valid = row < (seq_len - c * chunk)
            q32 = jnp.where(valid, q_ref[0, :, kcols].astype(F32), 0.0)
            k32 = jnp.where(valid, k_ref[0, :, kcols].astype(F32), 0.0)
            q = q32.astype(BF16)
            k = k32.astype(BF16)
            v = jnp.where(valid, v_ref[0, :, vcols].astype(F32), 0.0).astype(BF16)
            lg = jnp.where(valid, lg_ref[0, :, kcols], 0.0) * LOG2E
        else:
            q, k, v = q_ref[0, :, kcols], k_ref[0, :, kcols], v_ref[0, :, vcols]
            q32 = q.astype(F32)
            k32 = k.astype(F32)
            lg = lg_ref[0, :, kcols] * LOG2E

        lg_hi = lg.astype(BF16)
        lg_lo = (lg - lg_hi.astype(F32)).astype(BF16)
        tri = tri_ref[...]
        b = _dot(tri, lg_hi) + _dot(tri, lg_lo)

        n_levels = chunk.bit_length() - 1
        a = jnp.sum(q32 * k32, axis=-1, keepdims=True) * lm_ref[n_levels]
        for lev in range(n_levels):
            s = 1 << lev
            if s == 1:
                e = jnp.where((row & 1) == 1, lg, 0.0)
            elif s == 2:
                nxt = pltpu.roll(lg, chunk - 1, 0)
                prv = pltpu.roll(lg, 1, 0)
                pos = row & 3
                e = jnp.where(pos == 0, nxt, jnp.where(pos == 2, lg, jnp.where(pos == 3, lg + prv, 0.0)))
            else:
                b3 = b.reshape(chunk // (2 * s), 2 * s, dk)
                e = (-jnp.abs(b3 - b3[:, s - 1:s, :])).reshape(chunk, dk)
            f = jnp.exp2(e).astype(BF16)
            a = a + _dot_nt(q * f, k * f) * lm_ref[lev]

        b_last = b[chunk - 1:chunk, :]
        qs = q * jnp.exp2(b).astype(BF16)
        ks = k * jnp.exp2(b_last - b).astype(BF16)
        st = st_ref[head]
        o = (_dot(a.astype(BF16), v) + _dot_nt(qs, st.astype(BF16))) * q_scale
        st_ref[head] = st * jnp.exp2(b_last) + _dot_tn(v, ks)

        o = _rms_norm(o, hn_ref[...])
        r = r_ref[0, :, vcols].astype(F32)
        o_ref[0, :, vcols] = (o * (r * jax.nn.sigmoid(r))).astype(o_ref.dtype)

    def chunk_step(partial):
        for head in range(n_heads):
            head_step(head, partial)

    if n_full > 0:
        pl.when(c < n_full)(functools.partial(chunk_step, False))
    if seq_len % chunk:
        pl.when(c == n_full)(functools.partial(chunk_step, True))


def _gla_core(proj, lg, head_norm, batch, seq_len):
    dk_all = lg.shape[-1]
    dk = dk_all // GLA_HEADS
    dv_all = (proj.shape[-1] - 2 * dk_all) // 2
    dv = dv_all // GLA_HEADS
    proj = proj.reshape(batch, seq_len, proj.shape[-1])
    lg = lg.reshape(batch, seq_len, dk_all)
    c = GLA_CHUNK
    tri = jnp.asarray(np.tril(np.ones((c, c), np.float32)), BF16)
    level_masks = jnp.asarray(_gla_level_masks(c))
    assert dv_all == 2 * dk_all
    out = pl.pallas_call(
        functools.partial(_gla_core_kernel, seq_len=seq_len, q_scale=dk ** -0.5),
        grid=(batch, pl.cdiv(seq_len, c)),
        in_specs=[
            pl.BlockSpec((1, c, dk_all), lambda b, i: (b, i, 0)),
            pl.BlockSpec((1, c, dk_all), lambda b, i: (b, i, 1)),
            pl.BlockSpec((1, c, dv_all), lambda b, i: (b, i, 1)),
            pl.BlockSpec((1, c, dv_all), lambda b, i: (b, i, 2)),
            pl.BlockSpec((1, c, dk_all), lambda b, i: (b, i, 0)),
            pl.BlockSpec((c, c), lambda b, i: (0, 0)),
            pl.BlockSpec(level_masks.shape, lambda b, i: (0, 0, 0)),
            pl.BlockSpec((1, dv), lambda b, i: (0, 0)),
        ],
        out_specs=pl.BlockSpec((1, c, dv_all), lambda b, i: (b, i, 0)),
        out_shape=jax.ShapeDtypeStruct((batch, seq_len, dv_all), BF16),
        scratch_shapes=[pltpu.VMEM((GLA_HEADS, dv, dk), F32)],
        compiler_params=_params("parallel", "arbitrary"),
        name="gla_core",
    )(proj, proj, proj, proj, lg, tri, level_masks, head_norm.reshape(1, dv))
    return out.reshape(batch * seq_len, dv_all)


def _proj_add_kernel(x_ref, a_ref, w_ref, o_ref, wb_ref):
    @pl.when(pl.program_id(0) == 0)
    def _():
        wb_ref[...] = w_ref[...].astype(BF16)

    o_ref[...] = x_ref[...] + _dot(a_ref[...], wb_ref[...])


def _proj_add(x, a, w, layer):
    m, d = x.shape
    kdim = a.shape[1]
    return pl.pallas_call(
        _proj_add_kernel,
        grid=(m // ROW_TILE,),
        in_specs=[
            pl.BlockSpec((ROW_TILE, d), lambda i: (i, 0)),
            pl.BlockSpec((ROW_TILE, kdim), lambda i: (i, 0)),
            pl.BlockSpec((None, kdim, d), lambda i: (layer, 0, 0), pipeline_mode=pl.Buffered(1)),
        ],
        out_specs=pl.BlockSpec((ROW_TILE, d), lambda i: (i, 0)),
        out_shape=jax.ShapeDtypeStruct((m, d), F32),
        scratch_shapes=[pltpu.VMEM((kdim, d), BF16)],
        compiler_params=_params("arbitrary"),
        name="gla_out",
    )(x, a, w)


def _pool_kernel(x_ref, halo_ref, g_ref, w_ref, b_ref, sc_ref, o_ref, *, tiles_per_seq):
    i = pl.program_id(0)
    tile, d = x_ref.shape
    gw = w_ref.shape[1]
    first = (i % tiles_per_seq) == 0
    x = x_ref[...]
    h = _rms_norm(x, g_ref[...])
    h_halo = jnp.where(first, 0.0, _rms_norm(halo_ref[...], g_ref[...]))
    ext = jnp.concatenate([h_halo, h], axis=0)
    t = (i % tiles_per_seq) * tile + lax.broadcasted_iota(jnp.int32, (tile, 1), 0)
    for grp, win in enumerate(POOL_WINDOWS):
        cols = slice(grp * gw, (grp + 1) * gw)
        acc = ext[:, cols]
        step = 1
        while step < win:
            acc = acc + pltpu.roll(acc, step, 0)
            step *= 2
        inv_cnt = 1.0 / jnp.minimum(t + 1, win).astype(F32)
        pooled = acc[POOL_HALO:, :] * inv_cnt - h[:, cols]
        y = _dot(pooled.astype(BF16), w_ref[grp]) + b_ref[grp]
        o_ref[:, cols] = x[:, cols] + y * sc_ref[:, cols]


def _pool_mixer(x, g, w, b, scale, seq_len):
    m, d = x.shape
    groups, gw, _ = w.shape
    tiles_per_seq = seq_len // ROW_TILE
    halo_blocks = ROW_TILE // POOL_HALO
    return pl.pallas_call(
        functools.partial(_pool_kernel, tiles_per_seq=tiles_per_seq),
        grid=(m // ROW_TILE,),
        in_specs=[
            pl.BlockSpec((ROW_TILE, d), lambda i: (i, 0)),
            pl.BlockSpec((POOL_HALO, d), lambda i: (jnp.maximum(i * halo_blocks - 1, 0), 0)),
            pl.BlockSpec((1, d), lambda i: (0, 0)),
            pl.BlockSpec((groups, gw, gw), lambda i: (0, 0, 0)),
            pl.BlockSpec((groups, 1, gw), lambda i: (0, 0, 0)),
            pl.BlockSpec((1, d), lambda i: (0, 0)),
        ],
        out_specs=pl.BlockSpec((ROW_TILE, d), lambda i: (i, 0)),
        out_shape=jax.ShapeDtypeStruct((m, d), F32),
        compiler_params=_params("parallel"),
        name="pool_mixer",
    )(x, x, g.reshape(1, d), w, b.reshape(groups, 1, gw), scale.reshape(1, d))


def kernel(x, meta, ffn_norm, ffn_w_gate, ffn_w_up, ffn_w_down, gla_norm, gla_w_in, gla_w_lr, gla_b_lr,
           gla_head_norm, gla_w_out, pool_norm, pool_w, pool_b, pool_scale, final_norm):
    batch, seq, d = x.shape
    seq_len = seq + N_META
    rows = batch * seq_len
    depth = ffn_norm.shape[0]
    dk_all = gla_w_lr.shape[-1]
    dv_all = gla_w_out.shape[1]
    assert rows % ROW_TILE == 0 and seq_len % ROW_TILE == 0
    meta = meta.astype(x.dtype)
    w_in_t = jnp.swapaxes(gla_w_in, 1, 2)
    ffn = functools.partial(_ffn_half, meta=meta, w_gate=ffn_w_gate, w_up=ffn_w_up, w_down=ffn_w_down,
                            rows=rows, seq_len=seq_len)

    xs = x
    for i in range(depth):
        j = i // 2
        gla_layer = i % 2 == 0
        first, last = i == 0, i == depth - 1
        out = ffn(xs, g=ffn_norm[i, 0], layer=i, half=0, assemble=first, next_g=gla_norm[j] if gla_layer else None)
        if gla_layer:
            xs, hn = out
            lr0 = 2 * dk_all + dv_all
            pad_rank = ((0, V7X_LANES - GLA_GATE_RANK), (0, 0))
            w_a_t = jnp.pad(w_in_t[j, lr0:lr0 + GLA_GATE_RANK], pad_rank)
            w_b = jnp.pad(gla_w_lr[j], pad_rank).astype(BF16)
            proj, lg = _gla_in(hn, w_in_t, j, w_a_t, w_b, gla_b_lr[j], lr0, lr0 + GLA_GATE_RANK, dv_all)
            gated = _gla_core(proj, lg, gla_head_norm[j], batch, seq_len)
            xs = _proj_add(xs, gated, gla_w_out, j)
        else:
            xs = _pool_mixer(out, pool_norm[j], pool_w[j].astype(BF16), pool_b[j], pool_scale[j], seq_len)
        xs = ffn(xs, g=ffn_norm[i, 1], layer=i, half=1, strip=last, final_g=final_norm if last else None)
    return xs
```

```python
import functools
import math

import jax
import jax.numpy as jnp
import numpy as np
from jax import lax
from jax.experimental import pallas as pl
from jax.experimental.pallas import tpu as pltpu

F32 = jnp.float32
BF16 = jnp.bfloat16

EPS = 1e-6
N_META = 16
GLA_HEADS = 4
GLA_GATE_RANK = 16
GLA_GATE_NORM = 16.0
POOL_WINDOWS = (2, 4, 8, 16)
LOG2E = math.log2(math.e)

V7X_LANES = 128
V7X_BF16_SUBLANES = 16
V7X_VMEM_LIMIT_BYTES = 62 * 1024 * 1024

ROW_TILE = 688
FFN_ROW_TILE = 1376
FF_TILES = (512, 256)
FF_SUB = 256
GLA_CHUNK = 256
GLA_IN_TILE = 1024
GLA_LG_TILE = 256
POOL_HALO = 16
ROW_GROUP_UNROLL = 8


def _params(*semantics):
    return pltpu.CompilerParams(dimension_semantics=semantics,
                                vmem_limit_bytes=V7X_VMEM_LIMIT_BYTES)


def _rms_norm(x, g):
    return x * lax.rsqrt(jnp.mean(x * x, axis=-1, keepdims=True) + EPS) * g


def _dot(a, b):
    return jnp.dot(a, b, preferred_element_type=F32)


def _dot_nt(a, b):
    return lax.dot_general(a, b, (((1,), (1,)), ((), ())), preferred_element_type=F32)


def _dot_tn(a, b):
    return lax.dot_general(a, b, (((0,), (0,)), ((), ())), preferred_element_type=F32)


def _row_pieces(tile, seq_len, n_meta):
    period = math.lcm(tile, seq_len)
    phases = []
    for p in range(period // tile):
        pieces, r, end = [], p * tile, (p + 1) * tile
        while r < end:
            b, t = divmod(r, seq_len)
            if t < n_meta:
                n = min(n_meta - t, end - r)
                pieces.append(("meta", b, t, r - p * tile, n))
            else:
                n = min(seq_len - t, end - r)
                pieces.append(("seq", b, t - n_meta, r - p * tile, n))
            r += n
        phases.append(pieces)
    return phases, period // seq_len


def _tile_dma(action, to_vmem, t, slot, acc_ref, sems, main_hbm, meta_hbm, layout):
    tile = acc_ref.shape[1]

    def run(hbm, vmem, sem):
        copy = pltpu.make_async_copy(hbm, vmem, sem) if to_vmem else pltpu.make_async_copy(vmem, hbm, sem)
        if action == "start":
            copy.start()
        else:
            copy.wait()

    if layout is None:
        run(main_hbm.at[pl.ds(t * tile, tile)], acc_ref.at[slot], sems.at[slot, 0])
        return
    phases, batches = layout
    for p, pieces in enumerate(phases):
        @pl.when(t % len(phases) == p)
        def _():
            base = (t // len(phases)) * batches
            for idx, (kind, b, src, dst, n) in enumerate(pieces):
                vmem = acc_ref.at[slot, pl.ds(dst, n)]
                if kind == "seq":
                    run(main_hbm.at[base + b, pl.ds(src, n)], vmem, sems.at[slot, idx])
                elif to_vmem:
                    run(meta_hbm.at[pl.ds(src, n)], vmem, sems.at[slot, idx])


def _ffn_kernel(*refs, n_row_tiles, n_steps, in_layout, out_layout, final_norm, emit_h):
    if emit_h:
        (x_hbm, meta_hbm, g_ref, wg_ref, wu_ref, wd_ref, g2_ref, o_hbm, hn_hbm,
         acc_ref, h_ref, in_sem, out_sem, hn_sem) = refs
    else:
        (x_hbm, meta_hbm, g_ref, wg_ref, wu_ref, wd_ref, g2_ref, o_hbm,
         acc_ref, h_ref, in_sem, out_sem) = refs
    i = pl.program_id(0)
    j = pl.program_id(1)
    last_j = n_steps - 1
    tile = acc_ref.shape[1]
    group = V7X_BF16_SUBLANES
    n_groups = tile // group
    slot = i % 2
    load = functools.partial(_tile_dma, to_vmem=True, acc_ref=acc_ref, sems=in_sem, main_hbm=x_hbm,
                             meta_hbm=meta_hbm, layout=in_layout)
    store = functools.partial(_tile_dma, to_vmem=False, acc_ref=acc_ref, sems=out_sem, main_hbm=o_hbm,
                              meta_hbm=None, layout=out_layout)

    h_base = slot * tile if emit_h else 0

    def h_rows(first_row, rows):
        return pl.ds(pl.multiple_of(first_row, group), rows)

    def hn_copy(t, s):
        return pltpu.make_async_copy(h_ref.at[pl.ds(s * tile, tile)], hn_hbm.at[pl.ds(t * tile, tile)],
                                     hn_sem.at[s])

    @pl.when((i == 0) & (j == 0))
    def _():
        load("start", t=0, slot=0)

    @pl.when(j == 0)
    def _():
        load("wait", t=i, slot=slot)
        if emit_h:
            @pl.when(i >= 2)
            def _():
                hn_copy(i - 2, slot).wait()

        def rows_in(r, carry):
            x = acc_ref[slot, h_rows(r * group, group), :]
            h_ref[h_rows(h_base + r * group, group), :] = _rms_norm(x, g_ref[...]).astype(BF16)
            return carry

        lax.fori_loop(0, n_groups, rows_in, 0, unroll=ROW_GROUP_UNROLL)

    @pl.when((j == 1) & (i + 1 < n_row_tiles))
    def _():
        @pl.when(i >= 1)
        def _():
            store("wait", t=i - 1, slot=1 - slot)
        load("start", t=i + 1, slot=1 - slot)

    for c in range(wd_ref.shape[0] // FF_SUB):
        cols = slice(c * FF_SUB, (c + 1) * FF_SUB)
        h = h_ref[h_rows(h_base, tile), :]
        gate = _dot(h, wg_ref[:, cols].astype(BF16))
        up = _dot(h, wu_ref[:, cols].astype(BF16))
        act = (0.5 * gate * jax.nn.sigmoid(gate) * up).astype(BF16)
        acc_ref[slot] += _dot(act, wd_ref[cols, :].astype(BF16))

    @pl.when(j == last_j)
    def _():
        def rows_out(r, carry):
            rows = h_rows(r * group, group)
            hn = _rms_norm(acc_ref[slot, rows, :], g2_ref[...]).astype(BF16)
            h_ref[h_rows(h_base + r * group, group), :] = hn
            return carry

        if emit_h:
            lax.fori_loop(0, n_groups, rows_out, 0, unroll=ROW_GROUP_UNROLL)
            hn_copy(i, slot).start()
        if final_norm:
            for r in range(n_groups):
                rows = slice(r * group, (r + 1) * group)
                acc_ref[slot, rows, :] = _rms_norm(acc_ref[slot, rows, :], g2_ref[...])
        store("start", t=i, slot=slot)

    @pl.when((i == n_row_tiles - 1) & (j == last_j))
    def _():
        if n_row_tiles > 1:
            store("wait", t=i - 1, slot=1 - slot)
        store("wait", t=i, slot=slot)
        if emit_h:
            if n_row_tiles > 1:
                hn_copy(i - 1, 1 - slot).wait()
            hn_copy(i, slot).wait()


def _ffn_ff_tile(d, h_slots):
    fixed = 2 * FFN_ROW_TILE * d * 4 + h_slots * FFN_ROW_TILE * d * 2
    temps = 2 * FFN_ROW_TILE * FF_SUB * 4 + FFN_ROW_TILE * FF_SUB * 2 + d * FF_SUB * 2
    for ff_tile in FF_TILES:
        if fixed + temps + 3 * 2 * d * ff_tile * 4 <= V7X_VMEM_LIMIT_BYTES:
            return ff_tile
    raise ValueError("no D_FF chunk fits VMEM")


def _ffn_half(x, meta, g, w_gate, w_up, w_down, layer, half, *, rows, seq_len, assemble=False, strip=False,
              final_g=None, next_g=None):
    d = x.shape[-1]
    dff = w_gate.shape[-1]
    n_row_tiles = rows // FFN_ROW_TILE
    h_slots = 2 if next_g is not None else 1
    ff_tile = _ffn_ff_tile(d, h_slots)
    n_ff = dff // ff_tile
    assert n_row_tiles * FFN_ROW_TILE == rows and n_ff * ff_tile == dff and n_ff >= 2
    assert FFN_ROW_TILE % V7X_BF16_SUBLANES == 0 and not (final_g is not None and next_g is not None)
    layout = _row_pieces(FFN_ROW_TILE, seq_len, N_META)
    n_sems = max(len(p) for p in layout[0])
    g2 = final_g if final_g is not None else next_g
    g2 = jnp.ones((1, d), F32) if g2 is None else g2.reshape(1, d)
    batch = rows // seq_len
    out_shape = [jax.ShapeDtypeStruct((batch, seq_len - N_META, d) if strip else (rows, d), F32)]
    out_specs = [pl.BlockSpec(memory_space=pl.ANY)]
    if next_g is not None:
        out_shape.append(jax.ShapeDtypeStruct((rows, d), BF16))
        out_specs.append(pl.BlockSpec(memory_space=pl.ANY))
    out = pl.pallas_call(
        functools.partial(_ffn_kernel, n_row_tiles=n_row_tiles, n_steps=n_ff, in_layout=layout if assemble else None,
                          out_layout=layout if strip else None, final_norm=final_g is not None,
                          emit_h=next_g is not None),
        grid=(n_row_tiles, n_ff),
        in_specs=[
            pl.BlockSpec(memory_space=pl.ANY),
            pl.BlockSpec(memory_space=pl.ANY),
            pl.BlockSpec((1, d), lambda i, j: (0, 0)),
            pl.BlockSpec((None, None, d, ff_tile), lambda i, j: (layer, half, 0, j)),
            pl.BlockSpec((None, None, d, ff_tile), lambda i, j: (layer, half, 0, j)),
            pl.BlockSpec((None, None, ff_tile, d), lambda i, j: (layer, half, j, 0)),
            pl.BlockSpec((1, d), lambda i, j: (0, 0)),
        ],
        out_specs=out_specs,
        out_shape=out_shape,
        scratch_shapes=[
            pltpu.VMEM((2, FFN_ROW_TILE, d), F32),
            pltpu.VMEM((h_slots * FFN_ROW_TILE, d), BF16),
            pltpu.SemaphoreType.DMA((2, n_sems)),
            pltpu.SemaphoreType.DMA((2, n_sems)),
        ] + ([pltpu.SemaphoreType.DMA((2,))] if next_g is not None else []),
        compiler_params=_params("arbitrary", "arbitrary"),
        name="ffn_half",
    )(x, meta, g.reshape(1, d), w_gate, w_up, w_down, g2)
    return out if next_g is not None else out[0]


def _gla_in_kernel(h_ref, wt_ref, wa_ref, wb_ref, bb_ref, proj_ref, lg_ref, low_ref):
    @pl.when(pl.program_id(1) == 0)
    def _():
        low_ref[...] = _dot_nt(h_ref[...], wa_ref[...].astype(BF16)).astype(BF16)

    z = _dot(low_ref[...], wb_ref[...]) + bb_ref[...]
    log_sig = jnp.minimum(z, 0.0) - jnp.log(1.0 + jnp.exp(-jnp.abs(z)))
    lg_ref[...] = log_sig / GLA_GATE_NORM
    proj_ref[...] = _dot_nt(h_ref[...], wt_ref[0].astype(BF16)).astype(proj_ref.dtype)


def _gla_in(h, w_in_t, layer, w_a_t, w_b, b_b, qkv_cols, r_start, r_cols):
    m, d = h.shape
    dk = w_b.shape[1]
    n_qkv = qkv_cols // GLA_IN_TILE
    n_steps = n_qkv + r_cols // GLA_IN_TILE
    n_lg = dk // GLA_LG_TILE
    assert n_qkv * GLA_IN_TILE == qkv_cols and r_cols % GLA_IN_TILE == 0 and m % FFN_ROW_TILE == 0
    assert n_lg * GLA_LG_TILE == dk and n_lg <= n_steps and r_start % 8 == 0

    def w_rows(i, j):
        row = jnp.where(j < n_qkv, j * GLA_IN_TILE, r_start + (j - n_qkv) * GLA_IN_TILE)
        return (layer, pl.multiple_of(row, 8), 0)

    def lg_block(i, j):
        return jnp.minimum(j, n_lg - 1)

    return pl.pallas_call(
        _gla_in_kernel,
        grid=(m // FFN_ROW_TILE, n_steps),
        in_specs=[
            pl.BlockSpec((FFN_ROW_TILE, d), lambda i, j: (i, 0)),
            pl.BlockSpec((pl.Element(1), pl.Element(GLA_IN_TILE), pl.Element(d)), w_rows),
            pl.BlockSpec((V7X_LANES, d), lambda i, j: (0, 0)),
            pl.BlockSpec((V7X_LANES, GLA_LG_TILE), lambda i, j: (0, lg_block(i, j))),
            pl.BlockSpec((1, GLA_LG_TILE), lambda i, j: (0, lg_block(i, j))),
        ],
        out_specs=[
            pl.BlockSpec((FFN_ROW_TILE, GLA_IN_TILE), lambda i, j: (i, j)),
            pl.BlockSpec((FFN_ROW_TILE, GLA_LG_TILE), lambda i, j: (i, lg_block(i, j))),
        ],
        out_shape=[
            jax.ShapeDtypeStruct((m, qkv_cols + r_cols), BF16),
            jax.ShapeDtypeStruct((m, dk), F32),
        ],
        scratch_shapes=[pltpu.VMEM((FFN_ROW_TILE, V7X_LANES), BF16)],
        compiler_params=_params("parallel", "arbitrary"),
        name="gla_in",
    )(h, w_in_t, w_a_t, w_b, b_b.reshape(1, dk))


def _gla_level_masks(chunk):
    i = np.arange(chunk)[:, None]
    j = np.arange(chunk)[None, :]
    masks = []
    for lev in range(chunk.bit_length() - 1):
        same_block = (i >> (lev + 1)) == (j >> (lev + 1))
        masks.append(same_block & (((i >> lev) & 1) == 1) & (((j >> lev) & 1) == 0))
    masks.append(i == j)
    return np.stack(masks).astype(np.float32)


def _gla_core_kernel(q_ref, k_ref, v_ref, r_ref, lg_ref, tri_ref, lm_ref, hn_ref, *rest, tail, q_scale):
    lm_tail_ref, o_ref, st_ref = rest if tail else (None,) + rest
    c = pl.program_id(1)
    last = pl.num_programs(1) - 1
    n_heads = st_ref.shape[0]
    dk = q_ref.shape[2] // n_heads
    dv = v_ref.shape[2] // n_heads

    @pl.when(c == 0)
    def _():
        st_ref[...] = jnp.zeros_like(st_ref)

    def head_step(head, rows, masks_ref):
        kcols = slice(head * dk, (head + 1) * dk)
        vcols = slice(head * dv, (head + 1) * dv)
        row = lax.broadcasted_iota(jnp.int32, (rows, 1), 0)
        q, k, v = q_ref[0, :rows, kcols], k_ref[0, :rows, kcols], v_ref[0, :rows, vcols]
        lg = lg_ref[0, :rows, kcols] * LOG2E

        lg_hi = lg.astype(BF16)
        lg_lo = (lg - lg_hi.astype(F32)).astype(BF16)
        tri = tri_ref[:rows, :rows]
        b = _dot(tri, lg_hi) + _dot(tri, lg_lo)

        n_levels = rows.bit_length() - 1
        a = jnp.sum(q.astype(F32) * k.astype(F32), axis=-1, keepdims=True) * masks_ref[n_levels]
        for lev in range(n_levels):
            s = 1 << lev
            if s == 1:
                e = jnp.where((row & 1) == 1, lg, 0.0)
            elif s == 2:
                nxt = pltpu.roll(lg, rows - 1, 0)
                prv = pltpu.roll(lg, 1, 0)
                pos = row & 3
                e = jnp.where(pos == 0, nxt, jnp.where(pos == 2, lg, jnp.where(pos == 3, lg + prv, 0.0)))
            else:
                b3 = b.reshape(rows // (2 * s), 2 * s, dk)
                e = (-jnp.abs(b3 - b3[:, s - 1:s, :])).reshape(rows, dk)
            f = jnp.exp2(e).astype(BF16)
            a = a + _dot_nt(q * f, k * f) * masks_ref[lev]

        b_last = b[rows - 1:rows, :]
        qs = q * jnp.exp2(b).astype(BF16)
        ks = k * jnp.exp2(b_last - b).astype(BF16)
        st = st_ref[head]
        o = (_dot(a.astype(BF16), v) + _dot_nt(qs, st.astype(BF16))) * q_scale
        st_ref[head] = st * jnp.exp2(b_last) + _dot_tn(v, ks)

        o = _rms_norm(o, hn_ref[...])
        r = r_ref[0, :rows, vcols].astype(F32)
        o_ref[0, :rows, vcols] = (o * (r * jax.nn.sigmoid(r))).astype(o_ref.dtype)

    def chunk_step(rows, masks_ref):
        for head in range(n_heads):
            head_step(head, rows, masks_ref)

    if tail:
        pl.when(c < last)(functools.partial(chunk_step, q_ref.shape[1], lm_ref))
        pl.when(c == last)(functools.partial(chunk_step, tail, lm_tail_ref))
    else:
        chunk_step(q_ref.shape[1], lm_ref)


def _gla_core(proj, lg, head_norm, batch, seq_len):
    dk_all = lg.shape[-1]
    dk = dk_all // GLA_HEADS
    dv_all = (proj.shape[-1] - 2 * dk_all) // 2
    dv = dv_all // GLA_HEADS
    proj = proj.reshape(batch, seq_len, proj.shape[-1])
    lg = lg.reshape(batch, seq_len, dk_all)
    c = GLA_CHUNK
    tri = jnp.asarray(np.tril(np.ones((c, c), np.float32)), BF16)
    level_masks = jnp.asarray(_gla_level_masks(c))
    assert dv_all == 2 * dk_all
    tail = seq_len % c
    assert tail == 0 or (tail % V7X_BF16_SUBLANES == 0 and tail & (tail - 1) == 0 and seq_len > c)
    tail_masks = [jnp.asarray(_gla_level_masks(tail))] if tail else []
    out = pl.pallas_call(
        functools.partial(_gla_core_kernel, tail=tail, q_scale=dk ** -0.5),
        grid=(batch, pl.cdiv(seq_len, c)),
        in_specs=[
            pl.BlockSpec((1, c, dk_all), lambda b, i: (b, i, 0)),
            pl.BlockSpec((1, c, dk_all), lambda b, i: (b, i, 1)),
            pl.BlockSpec((1, c, dv_all), lambda b, i: (b, i, 1)),
            pl.BlockSpec((1, c, dv_all), lambda b, i: (b, i, 2)),
            pl.BlockSpec((1, c, dk_all), lambda b, i: (b, i, 0)),
            pl.BlockSpec((c, c), lambda b, i: (0, 0)),
            pl.BlockSpec(level_masks.shape, lambda b, i: (0, 0, 0)),
            pl.BlockSpec((1, dv), lambda b, i: (0, 0)),
        ] + [pl.BlockSpec(m.shape, lambda b, i: (0, 0, 0)) for m in tail_masks],
        out_specs=pl.BlockSpec((1, c, dv_all), lambda b, i: (b, i, 0)),
        out_shape=jax.ShapeDtypeStruct((batch, seq_len, dv_all), BF16),
        scratch_shapes=[pltpu.VMEM((GLA_HEADS, dv, dk), F32)],
        compiler_params=_params("parallel", "arbitrary"),
        name="gla_core",
    )(proj, proj, proj, proj, lg, tri, level_masks, head_norm.reshape(1, dv), *tail_masks)
    return out.reshape(batch * seq_len, dv_all)


def _proj_add_kernel(x_ref, a_ref, w_ref, o_ref, wb_ref):
    @pl.when(pl.program_id(0) == 0)
    def _():
        wb_ref[...] = w_ref[...].astype(BF16)

    o_ref[...] = x_ref[...] + _dot(a_ref[...], wb_ref[...])


def _proj_add(x, a, w, layer):
    m, d = x.shape
    kdim = a.shape[1]
    return pl.pallas_call(
        _proj_add_kernel,
        grid=(m // ROW_TILE,),
        in_specs=[
            pl.BlockSpec((ROW_TILE, d), lambda i: (i, 0)),
            pl.BlockSpec((ROW_TILE, kdim), lambda i: (i, 0)),
            pl.BlockSpec((None, kdim, d), lambda i: (layer, 0, 0), pipeline_mode=pl.Buffered(1)),
        ],
        out_specs=pl.BlockSpec((ROW_TILE, d), lambda i: (i, 0)),
        out_shape=jax.ShapeDtypeStruct((m, d), F32),
        scratch_shapes=[pltpu.VMEM((kdim, d), BF16)],
        compiler_params=_params("arbitrary"),
        name="gla_out",
    )(x, a, w)


def _pool_kernel(x_ref, halo_ref, g_ref, w_ref, b_ref, sc_ref, o_ref, *, tiles_per_seq):
    i = pl.program_id(0)
    tile, d = x_ref.shape
    gw = w_ref.shape[1]
    first = (i % tiles_per_seq) == 0
    x = x_ref[...]
    h = _rms_norm(x, g_ref[...])
    h_halo = jnp.where(first, 0.0, _rms_norm(halo_ref[...], g_ref[...]))
    ext = jnp.concatenate([h_halo, h], axis=0)
    t = (i % tiles_per_seq) * tile + lax.broadcasted_iota(jnp.int32, (tile, 1), 0)
    for grp, win in enumerate(POOL_WINDOWS):
        cols = slice(grp * gw, (grp + 1) * gw)
        acc = ext[:, cols]
        step = 1
        while step < win:
            acc = acc + pltpu.roll(acc, step, 0)
            step *= 2
        inv_cnt = 1.0 / jnp.minimum(t + 1, win).astype(F32)
        pooled = acc[POOL_HALO:, :] * inv_cnt - h[:, cols]
        y = _dot(pooled.astype(BF16), w_ref[grp]) + b_ref[grp]
        o_ref[:, cols] = x[:, cols] + y * sc_ref[:, cols]


def _pool_mixer(x, g, w, b, scale, seq_len):
    m, d = x.shape
    groups, gw, _ = w.shape
    tiles_per_seq = seq_len // ROW_TILE
    halo_blocks = ROW_TILE // POOL_HALO
    return pl.pallas_call(
        functools.partial(_pool_kernel, tiles_per_seq=tiles_per_seq),
        grid=(m // ROW_TILE,),
        in_specs=[
            pl.BlockSpec((ROW_TILE, d), lambda i: (i, 0)),
            pl.BlockSpec((POOL_HALO, d), lambda i: (jnp.maximum(i * halo_blocks - 1, 0), 0)),
            pl.BlockSpec((1, d), lambda i: (0, 0)),
            pl.BlockSpec((groups, gw, gw), lambda i: (0, 0, 0)),
            pl.BlockSpec((groups, 1, gw), lambda i: (0, 0, 0)),
            pl.BlockSpec((1, d), lambda i: (0, 0)),
        ],
        out_specs=pl.BlockSpec((ROW_TILE, d), lambda i: (i, 0)),
        out_shape=jax.ShapeDtypeStruct((m, d), F32),
        compiler_params=_params("parallel"),
        name="pool_mixer",
    )(x, x, g.reshape(1, d), w, b.reshape(groups, 1, gw), scale.reshape(1, d))


def kernel(x, meta, ffn_norm, ffn_w_gate, ffn_w_up, ffn_w_down, gla_norm, gla_w_in, gla_w_lr, gla_b_lr,
           gla_head_norm, gla_w_out, pool_norm, pool_w, pool_b, pool_scale, final_norm):
    batch, seq, d = x.shape
    seq_len = seq + N_META
    rows = batch * seq_len
    depth = ffn_norm.shape[0]
    dk_all = gla_w_lr.shape[-1]
    dv_all = gla_w_out.shape[1]
    assert rows % ROW_TILE == 0 and seq_len % ROW_TILE == 0
    meta = meta.astype(x.dtype)
    w_in_t = jnp.swapaxes(gla_w_in, 1, 2)
    ffn = functools.partial(_ffn_half, meta=meta, w_gate=ffn_w_gate, w_up=ffn_w_up, w_down=ffn_w_down,
                            rows=rows, seq_len=seq_len)

    xs = x
    for i in range(depth):
        j = i // 2
        gla_layer = i % 2 == 0
        first, last = i == 0, i == depth - 1
        out = ffn(xs, g=ffn_norm[i, 0], layer=i, half=0, assemble=first, next_g=gla_norm[j] if gla_layer else None)
        if gla_layer:
            xs, hn = out
            lr0 = 2 * dk_all + dv_all
            pad_rank = ((0, V7X_LANES - GLA_GATE_RANK), (0, 0))
            w_a_t = jnp.pad(w_in_t[j, lr0:lr0 + GLA_GATE_RANK], pad_rank)
            w_b = jnp.pad(gla_w_lr[j], pad_rank).astype(BF16)
            proj, lg = _gla_in(hn, w_in_t, j, w_a_t, w_b, gla_b_lr[j], lr0, lr0 + GLA_GATE_RANK, dv_all)
            gated = _gla_core(proj, lg, gla_head_norm[j], batch, seq_len)
            xs = _proj_add(xs, gated, gla_w_out, j)
        else:
            xs = _pool_mixer(out, pool_norm[j], pool_w[j].astype(BF16), pool_b[j], pool_scale[j], seq_len)
        xs = ffn(xs, g=ffn_norm[i, 1], layer=i, half=1, strip=last, final_g=final_norm if last else None)
    return xs
```

```python
import functools
import math

import jax
import jax.numpy as jnp
import numpy as np
from jax import lax
from jax.experimental import pallas as pl
from jax.experimental.pallas import tpu as pltpu

F32 = jnp.float32
BF16 = jnp.bfloat16

EPS = 1e-6
N_META = 16
GLA_HEADS = 4
GLA_GATE_RANK = 16
GLA_GATE_NORM = 16.0
POOL_WINDOWS = (2, 4, 8, 16)
LOG2E = math.log2(math.e)

V7X_LANES = 128
V7X_BF16_SUBLANES = 16
V7X_VMEM_LIMIT_BYTES = 62 * 1024 * 1024

ROW_TILE = 688
FFN_ROW_TILE = 1376
FF_TILES = (512, 256)
FF_SUB = 256
GLA_CHUNK = 256
GLA_IN_TILE = 1024
GLA_LG_TILE = 256
POOL_HALO = 16
ROW_GROUP_UNROLL = 8


def _params(*semantics):
    return pltpu.CompilerParams(dimension_semantics=semantics,
                                vmem_limit_bytes=V7X_VMEM_LIMIT_BYTES)


def _rms_norm(x, g):
    return x * lax.rsqrt(jnp.mean(x * x, axis=-1, keepdims=True) + EPS) * g


def _dot(a, b):
    return jnp.dot(a, b, preferred_element_type=F32)


def _dot_nt(a, b):
    return lax.dot_general(a, b, (((1,), (1,)), ((), ())), preferred_element_type=F32)


def _dot_tn(a, b):
    return lax.dot_general(a, b, (((0,), (0,)), ((), ())), preferred_element_type=F32)


def _row_pieces(tile, seq_len, n_meta):
    period = math.lcm(tile, seq_len)
    phases = []
    for p in range(period // tile):
        pieces, r, end = [], p * tile, (p + 1) * tile
        while r < end:
            b, t = divmod(r, seq_len)
            if t < n_meta:
                n = min(n_meta - t, end - r)
                pieces.append(("meta", b, t, r - p * tile, n))
            else:
                n = min(seq_len - t, end - r)
                pieces.append(("seq", b, t - n_meta, r - p * tile, n))
            r += n
        phases.append(pieces)
    return phases, period // seq_len


def _tile_dma(action, to_vmem, t, slot, acc_ref, sems, main_hbm, meta_hbm, layout):
    tile = acc_ref.shape[1]

    def run(hbm, vmem, sem):
        copy = pltpu.make_async_copy(hbm, vmem, sem) if to_vmem else pltpu.make_async_copy(vmem, hbm, sem)
        if action == "start":
            copy.start()
        else:
            copy.wait()

    if layout is None:
        run(main_hbm.at[pl.ds(t * tile, tile)], acc_ref.at[slot], sems.at[slot, 0])
        return
    phases, batches = layout
    for p, pieces in enumerate(phases):
        @pl.when(t % len(phases) == p)
        def _():
            base = (t // len(phases)) * batches
            for idx, (kind, b, src, dst, n) in enumerate(pieces):
                vmem = acc_ref.at[slot, pl.ds(dst, n)]
                if kind == "seq":
                    run(main_hbm.at[base + b, pl.ds(src, n)], vmem, sems.at[slot, idx])
                elif to_vmem:
                    run(meta_hbm.at[pl.ds(src, n)], vmem, sems.at[slot, idx])


def _ffn_kernel(*refs, n_row_tiles, n_steps, in_layout, out_layout, final_norm, emit_h):
    if emit_h:
        (x_hbm, meta_hbm, g_ref, wg_ref, wu_ref, wd_ref, g2_ref, o_hbm, hn_hbm,
         acc_ref, h_ref, in_sem, out_sem, hn_sem) = refs
    else:
        (x_hbm, meta_hbm, g_ref, wg_ref, wu_ref, wd_ref, g2_ref, o_hbm,
         acc_ref, h_ref, in_sem, out_sem) = refs
    i = pl.program_id(0)
    j = pl.program_id(1)
    last_j = n_steps - 1
    tile = acc_ref.shape[1]
    group = V7X_BF16_SUBLANES
    n_groups = tile // group
    slot = i % 2
    load = functools.partial(_tile_dma, to_vmem=True, acc_ref=acc_ref, sems=in_sem, main_hbm=x_hbm,
                             meta_hbm=meta_hbm, layout=in_layout)
    store = functools.partial(_tile_dma, to_vmem=False, acc_ref=acc_ref, sems=out_sem, main_hbm=o_hbm,
                              meta_hbm=None, layout=out_layout)

    h_base = slot * tile if emit_h else 0

    def h_rows(first_row, rows):
        return pl.ds(pl.multiple_of(first_row, group), rows)

    def hn_copy(t, s):
        return pltpu.make_async_copy(h_ref.at[pl.ds(s * tile, tile)], hn_hbm.at[pl.ds(t * tile, tile)],
                                     hn_sem.at[s])

    @pl.when((i == 0) & (j == 0))
    def _():
        load("start", t=0, slot=0)

    @pl.when(j == 0)
    def _():
        load("wait", t=i, slot=slot)
        if emit_h:
            @pl.when(i >= 2)
            def _():
                hn_copy(i - 2, slot).wait()

        def rows_in(r, carry):
            x = acc_ref[slot, h_rows(r * group, group), :]
            h_ref[h_rows(h_base + r * group, group), :] = _rms_norm(x, g_ref[...]).astype(BF16)
            return carry

        lax.fori_loop(0, n_groups, rows_in, 0, unroll=ROW_GROUP_UNROLL)

    @pl.when((j == 1) & (i + 1 < n_row_tiles))
    def _():
        @pl.when(i >= 1)
        def _():
            store("wait", t=i - 1, slot=1 - slot)
        load("start", t=i + 1, slot=1 - slot)

    acts = []
    for c in range(wd_ref.shape[0] // FF_SUB):
        cols = slice(c * FF_SUB, (c + 1) * FF_SUB)
        h = h_ref[h_rows(h_base, tile), :]
        gate = _dot(h, wg_ref[:, cols].astype(BF16))
        up = _dot(h, wu_ref[:, cols].astype(BF16))
        act = (0.5 * gate * jax.nn.sigmoid(gate) * up).astype(BF16)
        if emit_h:
            acc_ref[slot] += _dot(act, wd_ref[cols, :].astype(BF16))
        acts.append(act)
    if not emit_h:
        acc_ref[slot] += _dot(jnp.concatenate(acts, axis=1), wd_ref[...].astype(BF16))

    @pl.when(j == last_j)
    def _():
        def rows_out(r, carry):
            rows = h_rows(r * group, group)
            hn = _rms_norm(acc_ref[slot, rows, :], g2_ref[...]).astype(BF16)
            h_ref[h_rows(h_base + r * group, group), :] = hn
            return carry

        if emit_h:
            lax.fori_loop(0, n_groups, rows_out, 0, unroll=ROW_GROUP_UNROLL)
            hn_copy(i, slot).start()
        if final_norm:
            for r in range(n_groups):
                rows = slice(r * group, (r + 1) * group)
                acc_ref[slot, rows, :] = _rms_norm(acc_ref[slot, rows, :], g2_ref[...])
        store("start", t=i, slot=slot)

    @pl.when((i == n_row_tiles - 1) & (j == last_j))
    def _():
        if n_row_tiles > 1:
            store("wait", t=i - 1, slot=1 - slot)
        store("wait", t=i, slot=slot)
        if emit_h:
            if n_row_tiles > 1:
                hn_copy(i - 1, 1 - slot).wait()
            hn_copy(i, slot).wait()


def _ffn_ff_tile(d, h_slots):
    fixed = 2 * FFN_ROW_TILE * d * 4 + h_slots * FFN_ROW_TILE * d * 2
    temps = 2 * FFN_ROW_TILE * FF_SUB * 4 + FFN_ROW_TILE * FF_SUB * 2 + d * FF_SUB * 2
    for ff_tile in FF_TILES:
        if fixed + temps + 3 * 2 * d * ff_tile * 4 <= V7X_VMEM_LIMIT_BYTES:
            return ff_tile
    raise ValueError("no D_FF chunk fits VMEM")


def _ffn_half(x, meta, g, w_gate, w_up, w_down, layer, half, *, rows, seq_len, assemble=False, strip=False,
              final_g=None, next_g=None):
    d = x.shape[-1]
    dff = w_gate.shape[-1]
    n_row_tiles = rows // FFN_ROW_TILE
    h_slots = 2 if next_g is not None else 1
    ff_tile = _ffn_ff_tile(d, h_slots)
    n_ff = dff // ff_tile
    assert n_row_tiles * FFN_ROW_TILE == rows and n_ff * ff_tile == dff and n_ff >= 2
    assert FFN_ROW_TILE % V7X_BF16_SUBLANES == 0 and not (final_g is not None and next_g is not None)
    layout = _row_pieces(FFN_ROW_TILE, seq_len, N_META)
    n_sems = max(len(p) for p in layout[0])
    g2 = final_g if final_g is not None else next_g
    g2 = jnp.ones((1, d), F32) if g2 is None else g2.reshape(1, d)
    batch = rows // seq_len
    out_shape = [jax.ShapeDtypeStruct((batch, seq_len - N_META, d) if strip else (rows, d), F32)]
    out_specs = [pl.BlockSpec(memory_space=pl.ANY)]
    if next_g is not None:
        out_shape.append(jax.ShapeDtypeStruct((rows, d), BF16))
        out_specs.append(pl.BlockSpec(memory_space=pl.ANY))
    out = pl.pallas_call(
        functools.partial(_ffn_kernel, n_row_tiles=n_row_tiles, n_steps=n_ff, in_layout=layout if assemble else None,
                          out_layout=layout if strip else None, final_norm=final_g is not None,
                          emit_h=next_g is not None),
        grid=(n_row_tiles, n_ff),
        in_specs=[
            pl.BlockSpec(memory_space=pl.ANY),
            pl.BlockSpec(memory_space=pl.ANY),
            pl.BlockSpec((1, d), lambda i, j: (0, 0)),
            pl.BlockSpec((None, None, d, ff_tile), lambda i, j: (layer, half, 0, j)),
            pl.BlockSpec((None, None, d, ff_tile), lambda i, j: (layer, half, 0, j)),
            pl.BlockSpec((None, None, ff_tile, d), lambda i, j: (layer, half, j, 0)),
            pl.BlockSpec((1, d), lambda i, j: (0, 0)),
        ],
        out_specs=out_specs,
        out_shape=out_shape,
        scratch_shapes=[
            pltpu.VMEM((2, FFN_ROW_TILE, d), F32),
            pltpu.VMEM((h_slots * FFN_ROW_TILE, d), BF16),
            pltpu.SemaphoreType.DMA((2, n_sems)),
            pltpu.SemaphoreType.DMA((2, n_sems)),
        ] + ([pltpu.SemaphoreType.DMA((2,))] if next_g is not None else []),
        compiler_params=_params("arbitrary", "arbitrary"),
        name="ffn_half",
    )(x, meta, g.reshape(1, d), w_gate, w_up, w_down, g2)
    return out if next_g is not None else out[0]


def _gla_in_kernel(h_ref, wt_ref, wa_ref, wb_ref, bb_ref, proj_ref, lg_ref, low_ref):
    @pl.when(pl.program_id(1) == 0)
    def _():
        low_ref[...] = _dot_nt(h_ref[...], wa_ref[...].astype(BF16)).astype(BF16)

    z = _dot(low_ref[...], wb_ref[...]) + bb_ref[...]
    log_sig = jnp.minimum(z, 0.0) - jnp.log(1.0 + jnp.exp(-jnp.abs(z)))
    lg_ref[...] = log_sig / GLA_GATE_NORM
    proj_ref[...] = _dot_nt(h_ref[...], wt_ref[0].astype(BF16)).astype(proj_ref.dtype)


def _gla_in(h, w_in_t, layer, w_a_t, w_b, b_b, qkv_cols, r_start, r_cols):
    m, d = h.shape
    dk = w_b.shape[1]
    n_qkv = qkv_cols // GLA_IN_TILE
    n_steps = n_qkv + r_cols // GLA_IN_TILE
    n_lg = dk // GLA_LG_TILE
    assert n_qkv * GLA_IN_TILE == qkv_cols and r_cols % GLA_IN_TILE == 0 and m % FFN_ROW_TILE == 0
    assert n_lg * GLA_LG_TILE == dk and n_lg <= n_steps and r_start % 8 == 0

    def w_rows(i, j):
        row = jnp.where(j < n_qkv, j * GLA_IN_TILE, r_start + (j - n_qkv) * GLA_IN_TILE)
        return (layer, pl.multiple_of(row, 8), 0)

    def lg_block(i, j):
        return jnp.minimum(j, n_lg - 1)

    return pl.pallas_call(
        _gla_in_kernel,
        grid=(m // FFN_ROW_TILE, n_steps),
        in_specs=[
            pl.BlockSpec((FFN_ROW_TILE, d), lambda i, j: (i, 0)),
            pl.BlockSpec((pl.Element(1), pl.Element(GLA_IN_TILE), pl.Element(d)), w_rows),
            pl.BlockSpec((V7X_LANES, d), lambda i, j: (0, 0)),
            pl.BlockSpec((V7X_LANES, GLA_LG_TILE), lambda i, j: (0, lg_block(i, j))),
            pl.BlockSpec((1, GLA_LG_TILE), lambda i, j: (0, lg_block(i, j))),
        ],
        out_specs=[
            pl.BlockSpec((FFN_ROW_TILE, GLA_IN_TILE), lambda i, j: (i, j)),
            pl.BlockSpec((FFN_ROW_TILE, GLA_LG_TILE), lambda i, j: (i, lg_block(i, j))),
        ],
        out_shape=[
            jax.ShapeDtypeStruct((m, qkv_cols + r_cols), BF16),
            jax.ShapeDtypeStruct((m, dk), F32),
        ],
        scratch_shapes=[pltpu.VMEM((FFN_ROW_TILE, V7X_LANES), BF16)],
        compiler_params=_params("parallel", "arbitrary"),
        name="gla_in",
    )(h, w_in_t, w_a_t, w_b, b_b.reshape(1, dk))


def _gla_level_masks(chunk):
    i = np.arange(chunk)[:, None]
    j = np.arange(chunk)[None, :]
    masks = []
    for lev in range(chunk.bit_length() - 1):
        same_block = (i >> (lev + 1)) == (j >> (lev + 1))
        masks.append(same_block & (((i >> lev) & 1) == 1) & (((j >> lev) & 1) == 0))
    masks.append(i == j)
    return np.stack(masks).astype(np.float32)


def _gla_core_kernel(q_ref, k_ref, v_ref, r_ref, lg_ref, tri_ref, lm_ref, hn_ref, *rest, tail, q_scale):
    lm_tail_ref, o_ref, st_ref = rest if tail else (None,) + rest
    c = pl.program_id(1)
    last = pl.num_programs(1) - 1
    n_heads = st_ref.shape[0]
    dk = q_ref.shape[2] // n_heads
    dv = v_ref.shape[2] // n_heads

    @pl.when(c == 0)
    def _():
        st_ref[...] = jnp.zeros_like(st_ref)

    def head_step(head, rows, masks_ref):
        kcols = slice(head * dk, (head + 1) * dk)
        vcols = slice(head * dv, (head + 1) * dv)
        row = lax.broadcasted_iota(jnp.int32, (rows, 1), 0)
        q, k, v = q_ref[0, :rows, kcols], k_ref[0, :rows, kcols], v_ref[0, :rows, vcols]
        lg = lg_ref[0, :rows, kcols] * LOG2E

        lg_hi = lg.astype(BF16)
        lg_lo = (lg - lg_hi.astype(F32)).astype(BF16)
        tri = tri_ref[:rows, :rows]
        b = _dot(tri, lg_hi) + _dot(tri, lg_lo)

        n_levels = rows.bit_length() - 1
        a = jnp.sum(q.astype(F32) * k.astype(F32), axis=-1, keepdims=True) * masks_ref[n_levels]
        for lev in range(n_levels):
            s = 1 << lev
            if s == 1:
                e = jnp.where((row & 1) == 1, lg, 0.0)
            elif s == 2:
                nxt = pltpu.roll(lg, rows - 1, 0)
                prv = pltpu.roll(lg, 1, 0)
                pos = row & 3
                e = jnp.where(pos == 0, nxt, jnp.where(pos == 2, lg, jnp.where(pos == 3, lg + prv, 0.0)))
            else:
                b3 = b.reshape(rows // (2 * s), 2 * s, dk)
                e = (-jnp.abs(b3 - b3[:, s - 1:s, :])).reshape(rows, dk)
            f = jnp.exp2(e).astype(BF16)
            a = a + _dot_nt(q * f, k * f) * masks_ref[lev]

        b_last = b[rows - 1:rows, :]
        qs = q * jnp.exp2(b).astype(BF16)
        ks = k * jnp.exp2(b_last - b).astype(BF16)
        st = st_ref[head]
        o = (_dot(a.astype(BF16), v) + _dot_nt(qs, st.astype(BF16))) * q_scale
        st_ref[head] = st * jnp.exp2(b_last) + _dot_tn(v, ks)

        o = _rms_norm(o, hn_ref[...])
        r = r_ref[0, :rows, vcols].astype(F32)
        o_ref[0, :rows, vcols] = (o * (r * jax.nn.sigmoid(r))).astype(o_ref.dtype)

    def chunk_step(rows, masks_ref):
        for head in range(n_heads):
            head_step(head, rows, masks_ref)

    if tail:
        pl.when(c < last)(functools.partial(chunk_step, q_ref.shape[1], lm_ref))
        pl.when(c == last)(functools.partial(chunk_step, tail, lm_tail_ref))
    else:
        chunk_step(q_ref.shape[1], lm_ref)


def _gla_core(proj, lg, head_norm, batch, seq_len):
    dk_all = lg.shape[-1]
    dk = dk_all // GLA_HEADS
    dv_all = (proj.shape[-1] - 2 * dk_all) // 2
    dv = dv_all // GLA_HEADS
    proj = proj.reshape(batch, seq_len, proj.shape[-1])
    lg = lg.reshape(batch, seq_len, dk_all)
    c = GLA_CHUNK
    tri = jnp.asarray(np.tril(np.ones((c, c), np.float32)), BF16)
    level_masks = jnp.asarray(_gla_level_masks(c))
    assert dv_all == 2 * dk_all
    tail = seq_len % c
    assert tail == 0 or (tail % V7X_BF16_SUBLANES == 0 and tail & (tail - 1) == 0 and seq_len > c)
    tail_masks = [jnp.asarray(_gla_level_masks(tail))] if tail else []
    out = pl.pallas_call(
        functools.partial(_gla_core_kernel, tail=tail, q_scale=dk ** -0.5),
        grid=(batch, pl.cdiv(seq_len, c)),
        in_specs=[
            pl.BlockSpec((1, c, dk_all), lambda b, i: (b, i, 0)),
            pl.BlockSpec((1, c, dk_all), lambda b, i: (b, i, 1)),
            pl.BlockSpec((1, c, dv_all), lambda b, i: (b, i, 1)),
            pl.BlockSpec((1, c, dv_all), lambda b, i: (b, i, 2)),
            pl.BlockSpec((1, c, dk_all), lambda b, i: (b, i, 0)),
            pl.BlockSpec((c, c), lambda b, i: (0, 0)),
            pl.BlockSpec(level_masks.shape, lambda b, i: (0, 0, 0)),
            pl.BlockSpec((1, dv), lambda b, i: (0, 0)),
        ] + [pl.BlockSpec(m.shape, lambda b, i: (0, 0, 0)) for m in tail_masks],
        out_specs=pl.BlockSpec((1, c, dv_all), lambda b, i: (b, i, 0)),
        out_shape=jax.ShapeDtypeStruct((batch, seq_len, dv_all), BF16),
        scratch_shapes=[pltpu.VMEM((GLA_HEADS, dv, dk), F32)],
        compiler_params=_params("parallel", "arbitrary"),
        name="gla_core",
    )(proj, proj, proj, proj, lg, tri, level_masks, head_norm.reshape(1, dv), *tail_masks)
    return out.reshape(batch * seq_len, dv_all)


def _proj_add_kernel(x_ref, a_ref, w_ref, o_ref, wb_ref):
    @pl.when(pl.program_id(0) == 0)
    def _():
        wb_ref[...] = w_ref[...].astype(BF16)

    o_ref[...] = x_ref[...] + _dot(a_ref[...], wb_ref[...])


def _proj_add(x, a, w, layer):
    m, d = x.shape
    kdim = a.shape[1]
    return pl.pallas_call(
        _proj_add_kernel,
        grid=(m // ROW_TILE,),
        in_specs=[
            pl.BlockSpec((ROW_TILE, d), lambda i: (i, 0)),
            pl.BlockSpec((ROW_TILE, kdim), lambda i: (i, 0)),
            pl.BlockSpec((None, kdim, d), lambda i: (layer, 0, 0), pipeline_mode=pl.Buffered(1)),
        ],
        out_specs=pl.BlockSpec((ROW_TILE, d), lambda i: (i, 0)),
        out_shape=jax.ShapeDtypeStruct((m, d), F32),
        scratch_shapes=[pltpu.VMEM((kdim, d), BF16)],
        compiler_params=_params("arbitrary"),
        name="gla_out",
    )(x, a, w)


def _pool_kernel(x_ref, halo_ref, g_ref, w_ref, b_ref, sc_ref, o_ref, *, tiles_per_seq):
    i = pl.program_id(0)
    tile, d = x_ref.shape
    gw = w_ref.shape[1]
    first = (i % tiles_per_seq) == 0
    x = x_ref[...]
    h = _rms_norm(x, g_ref[...])
    h_halo = jnp.where(first, 0.0, _rms_norm(halo_ref[...], g_ref[...]))
    ext = jnp.concatenate([h_halo, h], axis=0)
    t = (i % tiles_per_seq) * tile + lax.broadcasted_iota(jnp.int32, (tile, 1), 0)
    for grp, win in enumerate(POOL_WINDOWS):
        cols = slice(grp * gw, (grp + 1) * gw)
        acc = ext[:, cols]
        step = 1
        while step < win:
            acc = acc + pltpu.roll(acc, step, 0)
            step *= 2
        inv_cnt = 1.0 / jnp.minimum(t + 1, win).astype(F32)
        pooled = acc[POOL_HALO:, :] * inv_cnt - h[:, cols]
        y = _dot(pooled.astype(BF16), w_ref[grp]) + b_ref[grp]
        o_ref[:, cols] = x[:, cols] + y * sc_ref[:, cols]


def _pool_mixer(x, g, w, b, scale, seq_len):
    m, d = x.shape
    groups, gw, _ = w.shape
    tiles_per_seq = seq_len // ROW_TILE
    halo_blocks = ROW_TILE // POOL_HALO
    return pl.pallas_call(
        functools.partial(_pool_kernel, tiles_per_seq=tiles_per_seq),
        grid=(m // ROW_TILE,),
        in_specs=[
            pl.BlockSpec((ROW_TILE, d), lambda i: (i, 0)),
            pl.BlockSpec((POOL_HALO, d), lambda i: (jnp.maximum(i * halo_blocks - 1, 0), 0)),
            pl.BlockSpec((1, d), lambda i: (0, 0)),
            pl.BlockSpec((groups, gw, gw), lambda i: (0, 0, 0)),
            pl.BlockSpec((groups, 1, gw), lambda i: (0, 0, 0)),
            pl.BlockSpec((1, d), lambda i: (0, 0)),
        ],
        out_specs=pl.BlockSpec((ROW_TILE, d), lambda i: (i, 0)),
        out_shape=jax.ShapeDtypeStruct((m, d), F32),
        compiler_params=_params("parallel"),
        name="pool_mixer",
    )(x, x, g.reshape(1, d), w, b.reshape(groups, 1, gw), scale.reshape(1, d))


def kernel(x, meta, ffn_norm, ffn_w_gate, ffn_w_up, ffn_w_down, gla_norm, gla_w_in, gla_w_lr, gla_b_lr,
           gla_head_norm, gla_w_out, pool_norm, pool_w, pool_b, pool_scale, final_norm):
    batch, seq, d = x.shape
    seq_len = seq + N_META
    rows = batch * seq_len
    depth = ffn_norm.shape[0]
    dk_all = gla_w_lr.shape[-1]
    dv_all = gla_w_out.shape[1]
    assert rows % ROW_TILE == 0 and seq_len % ROW_TILE == 0
    meta = meta.astype(x.dtype)
    w_in_t = jnp.swapaxes(gla_w_in, 1, 2)
    ffn = functools.partial(_ffn_half, meta=meta, w_gate=ffn_w_gate, w_up=ffn_w_up, w_down=ffn_w_down,
                            rows=rows, seq_len=seq_len)

    xs = x
    for i in range(depth):
        j = i // 2
        gla_layer = i % 2 == 0
        first, last = i == 0, i == depth - 1
        out = ffn(xs, g=ffn_norm[i, 0], layer=i, half=0, assemble=first, next_g=gla_norm[j] if gla_layer else None)
        if gla_layer:
            xs, hn = out
            lr0 = 2 * dk_all + dv_all
            pad_rank = ((0, V7X_LANES - GLA_GATE_RANK), (0, 0))
            w_a_t = jnp.pad(w_in_t[j, lr0:lr0 + GLA_GATE_RANK], pad_rank)
            w_b = jnp.pad(gla_w_lr[j], pad_rank).astype(BF16)
            proj, lg = _gla_in(hn, w_in_t, j, w_a_t, w_b, gla_b_lr[j], lr0, lr0 + GLA_GATE_RANK, dv_all)
            gated = _gla_core(proj, lg, gla_head_norm[j], batch, seq_len)
            xs = _proj_add(xs, gated, gla_w_out, j)
        else:
            xs = _pool_mixer(out, pool_norm[j], pool_w[j].astype(BF16), pool_b[j], pool_scale[j], seq_len)
        xs = ffn(xs, g=ffn_norm[i, 1], layer=i, half=1, strip=last, final_g=final_norm if last else None)
    return xs
```
